```python
import math
import jax, jax.numpy as jnp
from jax import lax
import numpy as np

D_MODEL = 1024
BATCH = 8
SEQ = 4096
DEPTH = 2

CHUNK = 64
Q_BLOCK = 128
POOL_WINDOWS = (2, 4, 8, 16)
N_POOL_GROUPS = len(POOL_WINDOWS)
POOL_GROUP_DIM = D_MODEL // 8
POOL_DIM = N_POOL_GROUPS * POOL_GROUP_DIM
N_HEADS = 8
HEAD_DIM = D_MODEL // 16
V_HEAD_DIM = 2 * HEAD_DIM
QK_DIM = N_HEADS * 2 * HEAD_DIM
V_DIM = N_HEADS * V_HEAD_DIM
IN_DIM = POOL_DIM + 2 * QK_DIM + V_DIM
D_FF = 4 * D_MODEL
N_BRANCHES = 2
EPS = 1e-6

kernel_name = "hybrid_pool_diffattn_gated_encoder"


def rmsnorm(x, g):
    xf = x.astype(jnp.float32)
    y = xf * lax.rsqrt(jnp.mean(xf * xf, axis=-1, keepdims=True) + EPS)
    return (y * g.astype(jnp.float32)).astype(x.dtype)


def head_rmsnorm(x, g):
    xf = x.astype(jnp.float32)
    return xf * lax.rsqrt(jnp.mean(xf * xf, axis=-1, keepdims=True) + EPS) * g.astype(jnp.float32)


def alibi_slopes(n):
    return jnp.asarray(np.array([2.0 ** (-8.0 * (i + 1) / n) for i in range(n)], dtype=np.float32))


def pool_mixer(u, w_grp, scale):
    B, S, _ = u.shape
    uf = u.astype(jnp.float32).reshape(B, S, N_POOL_GROUPS, POOL_GROUP_DIM)
    csum = jnp.pad(jnp.cumsum(uf, axis=1), ((0, 0), (1, 0), (0, 0), (0, 0)))
    t = jnp.arange(S)
    pooled = []
    for g, w in enumerate(POOL_WINDOWS):
        lo = jnp.maximum(t + 1 - w, 0)
        win_sum = csum[:, 1:, g] - csum[:, lo, g]
        cnt = (t + 1 - lo).astype(jnp.float32)
        pooled.append(win_sum / cnt[None, :, None])
    mixed = jnp.stack(pooled, axis=2) - uf
    y = jnp.einsum('bsgc,gcd->bsgd', mixed, w_grp.astype(jnp.float32))
    return (y.reshape(B, S, POOL_DIM) * scale.astype(jnp.float32)).astype(u.dtype)


def diff_attention(q, k, v, g_q, g_k, lam, g_sub, lambda_init):
    B, S = q.shape[0], q.shape[1]
    qn = head_rmsnorm(q, g_q) * (HEAD_DIM ** -0.5)
    kn = head_rmsnorm(k, g_k)
    vf = v.astype(jnp.float32)
    lam = lam.astype(jnp.float32)
    slopes = alibi_slopes(N_HEADS)
    pos = jnp.arange(S)
    key_chunk = pos // CHUNK
    nb = S // Q_BLOCK
    q_blocks = qn.reshape(B, nb, Q_BLOCK, N_HEADS, 2, HEAD_DIM).transpose(1, 0, 2, 3, 4, 5)
    q_pos = pos.reshape(nb, Q_BLOCK)
    neg = jnp.finfo(jnp.float32).min

    def block(args):
        qblk, qp = args
        s = jnp.einsum('bqhid,bkhid->bihqk', qblk, kn)
        dist = jnp.abs(qp[:, None] - pos[None, :]).astype(jnp.float32)
        bias = -slopes[:, None, None] * dist
        allowed = key_chunk[None, :] <= (qp // CHUNK)[:, None]
        s = jnp.where(allowed, s + bias, neg)
        p = jax.nn.softmax(s, axis=-1)
        a = p[:, 0] - lam * p[:, 1]
        return jnp.einsum('bhqk,bkhe->bqhe', a, vf)

    o = lax.map(block, (q_blocks, q_pos))
    o = o.transpose(1, 0, 2, 3, 4).reshape(B, S, N_HEADS, V_HEAD_DIM)
    o = head_rmsnorm(o, g_sub) * (1.0 - lambda_init)
    return o.reshape(B, S, V_DIM).astype(v.dtype)


def setup_inputs(seed: int = 0) -> dict:
    key = jax.random.key(seed)
    ks = jax.random.split(key, 20)
    f32 = jnp.float32

    def nrm(k, shape, fan_in):
        return jax.random.normal(k, shape, f32) * (fan_in ** -0.5)

    def gain(k, shape):
        return 1.0 + 0.05 * jax.random.normal(k, shape, f32)

    return {
        "x": jax.random.normal(ks[0], (BATCH, SEQ, D_MODEL), f32),
        "g_mix": gain(ks[1], (DEPTH, D_MODEL)),
        "w_in": nrm(ks[2], (DEPTH, D_MODEL, IN_DIM), D_MODEL),
        "w_pool_grp": nrm(ks[3], (DEPTH, N_POOL_GROUPS, POOL_GROUP_DIM, POOL_GROUP_DIM), POOL_GROUP_DIM),
        "pool_scale": gain(ks[4], (DEPTH, POOL_DIM)),
        "g_q": gain(ks[5], (DEPTH, HEAD_DIM)),
        "g_k": gain(ks[6], (DEPTH, HEAD_DIM)),
        "lambda_qk": 0.1 * jax.random.normal(ks[7], (DEPTH, 4, HEAD_DIM), f32),
        "g_sub": gain(ks[8], (DEPTH, V_HEAD_DIM)),
        "w_branch_pool": nrm(ks[9], (DEPTH, POOL_DIM, D_MODEL), POOL_DIM),
        "w_branch_attn": nrm(ks[10], (DEPTH, V_DIM, D_MODEL), V_DIM),
        "w_gate": nrm(ks[11], (DEPTH, D_MODEL, N_BRANCHES * D_MODEL), D_MODEL),
        "b_gate": 0.01 * jax.random.normal(ks[12], (DEPTH, N_BRANCHES * D_MODEL), f32),
        "w_out": nrm(ks[13], (DEPTH, D_MODEL, D_MODEL), D_MODEL),
        "g_ffn": gain(ks[14], (DEPTH, D_MODEL)),
        "w_up": nrm(ks[15], (DEPTH, D_MODEL, D_FF), D_MODEL),
        "w_down": nrm(ks[16], (DEPTH, D_FF, D_MODEL), D_FF),
    }


def reference(x, g_mix, w_in, w_pool_grp, pool_scale, g_q, g_k, lambda_qk, g_sub,
              w_branch_pool, w_branch_attn, w_gate, b_gate, w_out, g_ffn, w_up, w_down):
    B, S, _ = x.shape
    for l in range(DEPTH):
        lambda_init = 0.8 - 0.6 * math.exp(-0.3 * l)
        h = rmsnorm(x, g_mix[l])
        z = h @ w_in[l]
        u_pool = z[..., :POOL_DIM]
        q = z[..., POOL_DIM:POOL_DIM + QK_DIM].reshape(B, S, N_HEADS, 2, HEAD_DIM)
        k = z[..., POOL_DIM + QK_DIM:POOL_DIM + 2 * QK_DIM].reshape(B, S, N_HEADS, 2, HEAD_DIM)
        v = z[..., POOL_DIM + 2 * QK_DIM:].reshape(B, S, N_HEADS, V_HEAD_DIM)

        y_pool = pool_mixer(u_pool, w_pool_grp[l], pool_scale[l])
        lq = lambda_qk[l].astype(jnp.float32)
        lam = jnp.exp(jnp.sum(lq[0] * lq[1])) - jnp.exp(jnp.sum(lq[2] * lq[3])) + lambda_init
        y_attn = diff_attention(q, k, v, g_q[l], g_k[l], lam, g_sub[l], lambda_init)

        gates = jax.nn.sigmoid(h @ w_gate[l] + b_gate[l])
        merged = (gates[..., :D_MODEL] * (y_pool @ w_branch_pool[l])
                  + gates[..., D_MODEL:] * (y_attn @ w_branch_attn[l]))
        x = x + merged @ w_out[l]

        h2 = rmsnorm(x, g_ffn[l])
        x = x + jnp.square(jax.nn.relu(h2 @ w_up[l])) @ w_down[l]
    return x
```

```python
import functools
import math

import jax
import jax.numpy as jnp
from jax import lax
from jax.experimental import pallas as pl
from jax.experimental.pallas import tpu as pltpu

D_MODEL = 1024
CHUNK = 64
POOL_WINDOWS = (2, 4, 8, 16)
N_POOL_GROUPS = len(POOL_WINDOWS)
POOL_GROUP_DIM = 128
POOL_DIM = N_POOL_GROUPS * POOL_GROUP_DIM
N_HEADS = 8
HEAD_DIM = 64
V_HEAD_DIM = 2 * HEAD_DIM
QK_DIM = N_HEADS * 2 * HEAD_DIM
V_DIM = N_HEADS * V_HEAD_DIM
D_FF = 4 * D_MODEL
EPS = 1e-6

MXU_COLS_V7X = 256
VMEM_LIMIT_BYTES = 56 * 1024 * 1024

TOKEN_TILE = 512
ATTN_TILE = 256
POOL_HALO = 16
FF_CHUNK = 1024
MASK_VALUE = -1e30


def _const_spec(shape):
    nd = len(shape)
    return pl.BlockSpec(shape, lambda *_: (0,) * nd, pipeline_mode=pl.Buffered(1))


def _proj_kernel(x_ref, g_ref, wpk_ref, wqvt_ref, wg_ref, bg_ref, gk_ref, e_ref,
                 u_ref, kn_ref, qt_ref, vt_ref, gate_ref):
    xf = x_ref[...]
    ms = jnp.mean(xf * xf, axis=-1, keepdims=True)
    h = (xf * lax.rsqrt(ms + EPS) * g_ref[...]).astype(jnp.bfloat16)

    zpk = jnp.dot(h, wpk_ref[...], preferred_element_type=jnp.float32)
    u_ref[...] = zpk[:, :POOL_DIM].astype(u_ref.dtype)

    zk = zpk[:, POOL_DIM:]
    sq = zk * zk
    sq_hi = sq.astype(jnp.bfloat16)
    sq_lo = (sq - sq_hi.astype(jnp.float32)).astype(jnp.bfloat16)
    e = e_ref[...]
    parts = []
    for c in range(QK_DIM // MXU_COLS_V7X):
        sl = slice(c * MXU_COLS_V7X, (c + 1) * MXU_COLS_V7X)
        parts.append(jnp.dot(sq_hi[:, sl], e, preferred_element_type=jnp.float32)
                     + jnp.dot(sq_lo[:, sl], e, preferred_element_type=jnp.float32))
    ssq = jnp.concatenate(parts, axis=1)
    kn = zk * lax.rsqrt(ssq * (1.0 / HEAD_DIM) + EPS) * gk_ref[...]
    kn_ref[...] = kn.astype(kn_ref.dtype)

    qvt = lax.dot_general(wqvt_ref[...], h, (((1,), (1,)), ((), ())),
                          preferred_element_type=jnp.float32)
    qt_ref[0] = qvt[:QK_DIM].astype(qt_ref.dtype)
    vt_ref[0] = qvt[QK_DIM:].astype(vt_ref.dtype)

    zg = jnp.dot(h, wg_ref[...], preferred_element_type=jnp.float32) + bg_ref[...]
    gate_ref[...] = (1.0 / (1.0 + jnp.exp(-zg))).astype(gate_ref.dtype)


def _proj_call(x2d, g_mix, w_pk, w_qvt, w_gate, b_gate, gk_row, e_blk, batch, seq):
    tokens = x2d.shape[0]
    tm = TOKEN_TILE
    nst = seq // tm
    bf = jnp.bfloat16
    row = lambda i: (i, 0)
    tr = lambda i: (i // nst, 0, i % nst)
    return pl.pallas_call(
        _proj_kernel,
        grid=(tokens // tm,),
        in_specs=[
            pl.BlockSpec((tm, D_MODEL), row),
            _const_spec((1, D_MODEL)),
            _const_spec(w_pk.shape),
            _const_spec(w_qvt.shape),
            _const_spec(w_gate.shape),
            _const_spec((1, 2 * D_MODEL)),
            _const_spec((1, QK_DIM)),
            _const_spec(e_blk.shape),
        ],
        out_specs=[
            pl.BlockSpec((tm, POOL_DIM), row),
            pl.BlockSpec((tm, QK_DIM), row),
            pl.BlockSpec((1, QK_DIM, tm), tr),
            pl.BlockSpec((1, V_DIM, tm), tr),
            pl.BlockSpec((tm, 2 * D_MODEL), row),
        ],
        out_shape=[
            jax.ShapeDtypeStruct((tokens, POOL_DIM), bf),
            jax.ShapeDtypeStruct((tokens, QK_DIM), bf),
            jax.ShapeDtypeStruct((batch, QK_DIM, seq), bf),
            jax.ShapeDtypeStruct((batch, V_DIM, seq), bf),
            jax.ShapeDtypeStruct((tokens, 2 * D_MODEL), bf),
        ],
        compiler_params=pltpu.CompilerParams(
            dimension_semantics=("arbitrary",), vmem_limit_bytes=VMEM_LIMIT_BYTES),
        name="proj",
    )(x2d, g_mix, w_pk, w_qvt, w_gate, b_gate, gk_row, e_blk)


def _attn_kernel(slopes_ref, lq_ref, gq_ref, gsub_ref, qt_ref, k_ref, vt_ref, o_ref,
                 wq_scr, acc_scr, m_scr, l_scr, bias_scr, *, seq, lambda_init):
    t = ATTN_TILE
    nq = seq // t
    slope = slopes_ref[pl.program_id(1)]

    r = lax.broadcasted_iota(jnp.int32, (t, 2 * t), 0)
    c = lax.broadcasted_iota(jnp.int32, (t, 2 * t), 1)
    c = jnp.where(c >= t, c - t, c)
    rf = r.astype(jnp.float32)
    cf = c.astype(jnp.float32)
    bias_scr[0] = slope * rf
    diag = slope * jnp.where(r <= c, rf, 2.0 * cf - rf)
    bias_scr[1] = jnp.where((r // CHUNK) <= (c // CHUNK), diag, MASK_VALUE)

    lq = lq_ref[...]
    lam = (jnp.exp(jnp.sum(lq[0:1] * lq[1:2], axis=1, keepdims=True))
           - jnp.exp(jnp.sum(lq[2:3] * lq[3:4], axis=1, keepdims=True)) + lambda_init)

    wq_scr[...] = jnp.zeros_like(wq_scr)
    gq = gq_ref[...]

    def q_step(qi, carry):
        q0 = pl.multiple_of(qi * t, t)
        qt = qt_ref[0, :, pl.ds(q0, t)].astype(jnp.float32)
        for half in range(2):
            qh = qt[half * HEAD_DIM:(half + 1) * HEAD_DIM]
            inv = lax.rsqrt(jnp.mean(qh * qh, axis=0, keepdims=True) + EPS)
            wq_scr[half * HEAD_DIM:(half + 1) * HEAD_DIM, half * t:(half + 1) * t] = (
                qh * inv * gq).astype(wq_scr.dtype)
        m_scr[...] = jnp.full_like(m_scr, MASK_VALUE)
        l_scr[...] = jnp.zeros_like(l_scr)
        acc_scr[...] = jnp.zeros_like(acc_scr)

        def kv_step(j, carry2):
            k0 = pl.multiple_of(j * t, t)
            kt = k_ref[0, pl.ds(k0, t), :]
            s = jnp.dot(kt, wq_scr[...], preferred_element_type=jnp.float32)
            is_diag = (j == qi).astype(jnp.int32)
            s = s + bias_scr[is_diag]
            shift = slope * (k0 - q0).astype(jnp.float32)
            m_old = m_scr[...]
            m_new = jnp.maximum(m_old, jnp.max(s, axis=0, keepdims=True) + shift)
            p = jnp.exp(s - (m_new - shift))
            alpha = jnp.exp(m_old - m_new)
            l_scr[...] = alpha * l_scr[...] + jnp.sum(p, axis=0, keepdims=True)
            vt = vt_ref[0, :, pl.ds(k0, t)]
            acc_scr[...] = alpha * acc_scr[...] + jnp.dot(
                vt, p.astype(jnp.bfloat16), preferred_element_type=jnp.float32)
            m_scr[...] = m_new
            return carry2

        lax.fori_loop(0, qi + 1, kv_step, 0)

        inv_l = 1.0 / l_scr[...]
        acc = acc_scr[...] * inv_l
        o = acc[:, :t] - lam * acc[:, t:]
        o = o * lax.rsqrt(jnp.mean(o * o, axis=0, keepdims=True) + EPS)
        o_ref[0, pl.ds(q0, t), :] = (o.T * gsub_ref[...]).astype(o_ref.dtype)
        return carry

    lax.fori_loop(0, nq, q_step, 0)


def _attn_call(slopes, lq, gq_col, gsub_row, qt, kn, vt, lambda_init):
    batch, _, seq = qt.shape
    t = ATTN_TILE
    kernel = functools.partial(_attn_kernel, seq=seq, lambda_init=lambda_init)
    return pl.pallas_call(
        kernel,
        grid=(batch, N_HEADS),
        in_specs=[
            pl.BlockSpec(memory_space=pltpu.SMEM),
            _const_spec(lq.shape),
            _const_spec(gq_col.shape),
            _const_spec(gsub_row.shape),
            pl.BlockSpec((1, 2 * HEAD_DIM, seq), lambda b, h: (b, h, 0)),
            pl.BlockSpec((1, seq, 2 * HEAD_DIM), lambda b, h: (b, 0, h)),
            pl.BlockSpec((1, V_HEAD_DIM, seq), lambda b, h: (b, h, 0)),
        ],
        out_specs=pl.BlockSpec((1, seq, V_HEAD_DIM), lambda b, h: (b, 0, h)),
        out_shape=jax.ShapeDtypeStruct((batch, seq, V_DIM), jnp.bfloat16),
        scratch_shapes=[
            pltpu.VMEM((2 * HEAD_DIM, 2 * t), jnp.bfloat16),
            pltpu.VMEM((V_HEAD_DIM, 2 * t), jnp.float32),
            pltpu.VMEM((1, 2 * t), jnp.float32),
            pltpu.VMEM((1, 2 * t), jnp.float32),
            pltpu.VMEM((2, t, 2 * t), jnp.float32),
        ],
        compiler_params=pltpu.CompilerParams(
            dimension_semantics=("arbitrary", "arbitrary"), vmem_limit_bytes=VMEM_LIMIT_BYTES),
        name="attn",
    )(slopes, lq, gq_col, gsub_row, qt, kn, vt)


def _mix_kernel(x_ref, u_ref, halo_ref, ya_ref, gate_ref, wpg_ref, ps_ref, wbp_ref, wba_ref,
                wout_ref, gffn_ref, wup_ref, wdown_ref, o_ref, *, tiles_per_seq):
    tm = TOKEN_TILE
    i = pl.program_id(0)
    tile_in_seq = i % tiles_per_seq

    halo = jnp.where(tile_in_seq == 0, 0.0, halo_ref[...].astype(jnp.float32))
    u = u_ref[...].astype(jnp.float32)
    ext = jnp.concatenate([halo, u], axis=0)
    pos = tile_in_seq * tm + lax.broadcasted_iota(jnp.int32, (tm, POOL_GROUP_DIM), 0)
    win = ext
    ys = []
    for g, w in enumerate(POOL_WINDOWS):
        win = win[:, POOL_GROUP_DIM * (1 if g else 0):]
        win = win + pltpu.roll(win, w // 2, 0)
        cnt = jnp.minimum(pos + 1, w).astype(jnp.float32)
        ug = u[:, g * POOL_GROUP_DIM:(g + 1) * POOL_GROUP_DIM]
        mixed = win[POOL_HALO:, :POOL_GROUP_DIM] / cnt - ug
        ys.append(jnp.dot(mixed.astype(jnp.bfloat16), wpg_ref[g],
                          preferred_element_type=jnp.float32))
    y_pool = (jnp.concatenate(ys, axis=1) * ps_ref[...]).astype(jnp.bfloat16)

    bp = jnp.dot(y_pool, wbp_ref[...], preferred_element_type=jnp.float32)
    ba = jnp.dot(ya_ref[...], wba_ref[...], preferred_element_type=jnp.float32)
    gates = gate_ref[...].astype(jnp.float32)
    merged = gates[:, :D_MODEL] * bp + gates[:, D_MODEL:] * ba
    x1 = x_ref[...] + jnp.dot(merged.astype(jnp.bfloat16), wout_ref[...],
                              preferred_element_type=jnp.float32)

    ms = jnp.mean(x1 * x1, axis=-1, keepdims=True)
    h2 = (x1 * lax.rsqrt(ms + EPS) * gffn_ref[...]).astype(jnp.bfloat16)
    y = x1
    for cidx in range(D_FF // FF_CHUNK):
        sl = slice(cidx * FF_CHUNK, (cidx + 1) * FF_CHUNK)
        up = jnp.maximum(jnp.dot(h2, wup_ref[:, sl], preferred_element_type=jnp.float32), 0.0)
        y = y + jnp.dot((up * up).astype(jnp.bfloat16), wdown_ref[sl, :],
                        preferred_element_type=jnp.float32)
    o_ref[...] = y


def _mix_call(x2d, u, y_attn, gates, w_pg, pool_scale, w_bp, w_ba, w_out, g_ffn, w_up, w_down,
              seq):
    tokens = x2d.shape[0]
    tm = TOKEN_TILE
    row = lambda i: (i, 0)
    halo_blocks = tm // POOL_HALO
    kernel = functools.partial(_mix_kernel, tiles_per_seq=seq // tm)
    return pl.pallas_call(
        kernel,
        grid=(tokens // tm,),
        in_specs=[
            pl.BlockSpec((tm, D_MODEL), row),
            pl.BlockSpec((tm, POOL_DIM), row),
            pl.BlockSpec((POOL_HALO, POOL_DIM),
                         lambda i: (jnp.maximum(i * halo_blocks - 1, 0), 0)),
            pl.BlockSpec((tm, V_DIM), row),
            pl.BlockSpec((tm, 2 * D_MODEL), row),
            _const_spec(w_pg.shape),
            _const_spec((1, POOL_DIM)),
            _const_spec(w_bp.shape),
            _const_spec(w_ba.shape),
            _const_spec(w_out.shape),
            _const_spec((1, D_MODEL)),
            _const_spec(w_up.shape),
            _const_spec(w_down.shape),
        ],
        out_specs=pl.BlockSpec((tm, D_MODEL), row),
        out_shape=jax.ShapeDtypeStruct((tokens, D_MODEL), jnp.float32),
        compiler_params=pltpu.CompilerParams(
            dimension_semantics=("arbitrary",), vmem_limit_bytes=VMEM_LIMIT_BYTES),
        name="mix",
    )(x2d, u, u, y_attn, gates, w_pg, pool_scale, w_bp, w_ba, w_out, g_ffn, w_up, w_down)


def kernel(x, g_mix, w_in, w_pool_grp, pool_scale, g_q, g_k, lambda_qk, g_sub, w_branch_pool,
           w_branch_attn, w_gate, b_gate, w_out, g_ffn, w_up, w_down):
    batch, seq, d_model = x.shape
    depth = w_in.shape[0]
    assert d_model == D_MODEL and seq % TOKEN_TILE == 0 and seq % ATTN_TILE == 0
    assert TOKEN_TILE % POOL_HALO == 0 and POOL_HALO >= max(POOL_WINDOWS) - 1
    bf = jnp.bfloat16
    f32 = jnp.float32

    slopes = jnp.asarray([2.0 ** (-8.0 * (i + 1) / N_HEADS) for i in range(N_HEADS)], f32)
    grp = jnp.arange(MXU_COLS_V7X) // HEAD_DIM
    e_blk = (grp[:, None] == grp[None, :]).astype(bf)

    x2d = x.reshape(batch * seq, d_model)
    for l in range(depth):
        lambda_init = 0.8 - 0.6 * math.exp(-0.3 * l)
        k_lo, v_lo = POOL_DIM + QK_DIM, POOL_DIM + 2 * QK_DIM
        w_pk = jnp.concatenate([w_in[l][:, :POOL_DIM], w_in[l][:, k_lo:v_lo]], axis=1).astype(bf)
        w_qvt = jnp.concatenate([w_in[l][:, POOL_DIM:k_lo], w_in[l][:, v_lo:]], axis=1).T.astype(bf)
        gk_row = jnp.tile(g_k[l].astype(f32), 2 * N_HEADS).reshape(1, QK_DIM)
        gq_col = (g_q[l].astype(f32) * (HEAD_DIM ** -0.5)).reshape(HEAD_DIM, 1)
        gsub_row = (g_sub[l].astype(f32) * (1.0 - lambda_init)).reshape(1, V_HEAD_DIM)

        u, kn, qt, vt, gates = _proj_call(
            x2d, g_mix[l].reshape(1, d_model).astype(f32), w_pk, w_qvt, w_gate[l].astype(bf),
            b_gate[l].reshape(1, 2 * d_model).astype(f32), gk_row, e_blk, batch, seq)
        y_attn = _attn_call(slopes, lambda_qk[l].astype(f32), gq_col, gsub_row, qt,
                            kn.reshape(batch, seq, QK_DIM), vt, lambda_init)
        x2d = _mix_call(
            x2d, u, y_attn.reshape(batch * seq, V_DIM), gates, w_pool_grp[l].astype(bf),
            pool_scale[l].reshape(1, POOL_DIM).astype(f32), w_branch_pool[l].astype(bf),
            w_branch_attn[l].astype(bf), w_out[l].astype(bf),
            g_ffn[l].reshape(1, d_model).astype(f32), w_up[l].astype(bf), w_down[l].astype(bf), seq)
    return x2d.reshape(batch, seq, d_model)
```

```python
import functools
import math

import jax
import jax.numpy as jnp
from jax import lax
from jax.experimental import pallas as pl
from jax.experimental.pallas import tpu as pltpu

D_MODEL = 1024
CHUNK = 64
POOL_WINDOWS = (2, 4, 8, 16)
N_POOL_GROUPS = len(POOL_WINDOWS)
POOL_GROUP_DIM = 128
POOL_DIM = N_POOL_GROUPS * POOL_GROUP_DIM
N_HEADS = 8
HEAD_DIM = 64
V_HEAD_DIM = 2 * HEAD_DIM
QK_DIM = N_HEADS * 2 * HEAD_DIM
V_DIM = N_HEADS * V_HEAD_DIM
D_FF = 4 * D_MODEL
EPS = 1e-6
LOG2E = math.log2(math.e)

MXU_COLS_V7X = 256
BF16_ROWS_V7X = 16
VMEM_LIMIT_BYTES = 56 * 1024 * 1024

TOKEN_TILE = 512
ATTN_TILE = 256
DIAG_UNROLL = 4
PAIR_UNROLL = 8
POOL_HALO = 16
FF_CHUNK = 1024
MASK_VALUE = -1e30

K_AUG = 2 * HEAD_DIM + MXU_COLS_V7X // 2
N_POS_COLS = 3
V_AUG = V_HEAD_DIM + BF16_ROWS_V7X


def _const_spec(shape):
    nd = len(shape)
    return pl.BlockSpec(shape, lambda *_: (0,) * nd, pipeline_mode=pl.Buffered(1))


def _proj_kernel(x_ref, g_ref, wpk_ref, wqvt_ref, wg_ref, bg_ref, gk_ref, e_ref,
                 u_ref, kn_ref, qt_ref, vt_ref, gate_ref):
    xf = x_ref[...]
    ms = jnp.mean(xf * xf, axis=-1, keepdims=True)
    h = (xf * lax.rsqrt(ms + EPS) * g_ref[...]).astype(jnp.bfloat16)

    zpk = jnp.dot(h, wpk_ref[...], preferred_element_type=jnp.float32)
    u_ref[...] = zpk[:, :POOL_DIM].astype(u_ref.dtype)

    zk = zpk[:, POOL_DIM:]
    sq = zk * zk
    sq_hi = sq.astype(jnp.bfloat16)
    sq_lo = (sq - sq_hi.astype(jnp.float32)).astype(jnp.bfloat16)
    e = e_ref[...]
    parts = []
    for c in range(QK_DIM // MXU_COLS_V7X):
        sl = slice(c * MXU_COLS_V7X, (c + 1) * MXU_COLS_V7X)
        parts.append(jnp.dot(sq_hi[:, sl], e, preferred_element_type=jnp.float32)
                     + jnp.dot(sq_lo[:, sl], e, preferred_element_type=jnp.float32))
    ssq = jnp.concatenate(parts, axis=1)
    kn = zk * lax.rsqrt(ssq * (1.0 / HEAD_DIM) + EPS) * gk_ref[...]
    kn_ref[...] = kn.astype(kn_ref.dtype)

    qvt = lax.dot_general(wqvt_ref[...], h, (((1,), (1,)), ((), ())),
                          preferred_element_type=jnp.float32)
    qt_ref[0] = qvt[:QK_DIM].astype(qt_ref.dtype)
    vt_ref[0] = qvt[QK_DIM:].astype(vt_ref.dtype)

    zg = jnp.dot(h, wg_ref[...], preferred_element_type=jnp.float32) + bg_ref[...]
    gate_ref[...] = (1.0 / (1.0 + jnp.exp(-zg))).astype(gate_ref.dtype)


def _proj_call(x2d, g_mix, w_pk, w_qvt, w_gate, b_gate, gk_row, e_blk, batch, seq):
    tokens = x2d.shape[0]
    tm = TOKEN_TILE
    nst = seq // tm
    bf = jnp.bfloat16
    row = lambda i: (i, 0)
    tr = lambda i: (i // nst, 0, i % nst)
    return pl.pallas_call(
        _proj_kernel,
        grid=(tokens // tm,),
        in_specs=[
            pl.BlockSpec((tm, D_MODEL), row),
            _const_spec((1, D_MODEL)),
            _const_spec(w_pk.shape),
            _const_spec(w_qvt.shape),
            _const_spec(w_gate.shape),
            _const_spec((1, 2 * D_MODEL)),
            _const_spec((1, QK_DIM)),
            _const_spec(e_blk.shape),
        ],
        out_specs=[
            pl.BlockSpec((tm, POOL_DIM), row),
            pl.BlockSpec((tm, QK_DIM), row),
            pl.BlockSpec((1, QK_DIM, tm), tr),
            pl.BlockSpec((1, V_DIM, tm), tr),
            pl.BlockSpec((tm, 2 * D_MODEL), row),
        ],
        out_shape=[
            jax.ShapeDtypeStruct((tokens, POOL_DIM), bf),
            jax.ShapeDtypeStruct((tokens, QK_DIM), bf),
            jax.ShapeDtypeStruct((batch, QK_DIM, seq), bf),
            jax.ShapeDtypeStruct((batch, V_DIM, seq), bf),
            jax.ShapeDtypeStruct((tokens, 2 * D_MODEL), bf),
        ],
        compiler_params=pltpu.CompilerParams(
            dimension_semantics=("arbitrary",), vmem_limit_bytes=VMEM_LIMIT_BYTES),
        name="proj",
    )(x2d, g_mix, w_pk, w_qvt, w_gate, b_gate, gk_row, e_blk)


def _attn_kernel(slopes_ref, lq_ref, gq_ref, gsub_ref, qt_ref, k_ref, vt_ref, o_ref,
                 kaug, vaug, wq, acc, m_all, s_even, s_odd, p_even, p_odd, a_even, a_odd, dbias,
                 *, seq, lambda_init):
    t = ATTN_TILE
    nq = seq // t
    slope2 = slopes_ref[pl.program_id(1)] * LOG2E

    kaug[:, :2 * HEAD_DIM] = k_ref[0]
    row = lax.broadcasted_iota(jnp.int32, (t, K_AUG - 2 * HEAD_DIM), 0)
    lane = lax.broadcasted_iota(jnp.int32, (t, K_AUG - 2 * HEAD_DIM), 1)
    pos_cols = jnp.where(lane < N_POS_COLS, row.astype(jnp.float32), 0.0).astype(kaug.dtype)
    for i in range(nq):
        kaug[i * t:(i + 1) * t, 2 * HEAD_DIM:] = pos_cols
    vaug[:V_HEAD_DIM, :] = vt_ref[0]
    vaug[V_HEAD_DIM:, :] = jnp.ones((V_AUG - V_HEAD_DIM, seq), vaug.dtype)

    cvec = jnp.full((BF16_ROWS_V7X, 2 * t), slope2, jnp.float32)
    c_hi = cvec.astype(jnp.bfloat16).astype(jnp.float32)
    c_mid = (cvec - c_hi).astype(jnp.bfloat16).astype(jnp.float32)
    c_lo = cvec - c_hi - c_mid
    prow = lax.broadcasted_iota(jnp.int32, (BF16_ROWS_V7X, 2 * t), 0)
    pos_rows = jnp.where(prow == 0, c_hi, jnp.where(prow == 1, c_mid, jnp.where(prow == 2, c_lo, 0.0)))
    pos_rows = pos_rows.astype(wq.dtype)

    r = lax.broadcasted_iota(jnp.int32, (t, t), 0)
    c = lax.broadcasted_iota(jnp.int32, (t, t), 1)
    ahead = slope2 * (2.0 * (c - r).astype(jnp.float32))
    one_map = jnp.where((r // CHUNK) <= (c // CHUNK), jnp.where(r <= c, 0.0, ahead), MASK_VALUE)
    dbias[:, :t] = one_map
    dbias[:, t:] = one_map

    lq = lq_ref[...]
    lam = (jnp.exp(jnp.sum(lq[0:1] * lq[1:2], axis=1, keepdims=True))
           - jnp.exp(jnp.sum(lq[2:3] * lq[3:4], axis=1, keepdims=True)) + lambda_init)
    gq = gq_ref[...]

    def build_wq(qtile):
        q0 = pl.multiple_of(qtile * t, t)
        qt = qt_ref[0, :, pl.ds(q0, t)].astype(jnp.float32)
        zeros = jnp.zeros((HEAD_DIM, t), jnp.float32)
        for half in range(2):
            qh = qt[half * HEAD_DIM:(half + 1) * HEAD_DIM]
            qh = qh * lax.rsqrt(jnp.mean(qh * qh, axis=0, keepdims=True) + EPS) * gq
            both = jnp.concatenate([zeros, qh] if half else [qh, zeros], axis=1)
            wq[qtile, half * HEAD_DIM:(half + 1) * HEAD_DIM, :] = both.astype(wq.dtype)
        wq[qtile, 2 * HEAD_DIM:2 * HEAD_DIM + BF16_ROWS_V7X, :] = pos_rows
        wq[qtile, 2 * HEAD_DIM + BF16_ROWS_V7X:, :] = jnp.zeros(
            (K_AUG - 2 * HEAD_DIM - BF16_ROWS_V7X, 2 * t), wq.dtype)

    def per_tile_pair(fn):
        def body(i, carry):
            fn(2 * i)
            fn(2 * i + 1)
            return carry
        lax.fori_loop(0, nq // 2, body, 0)

    per_tile_pair(build_wq)

    def scores(qtile, ktile, s_dst):
        k0 = pl.multiple_of(ktile * t, t)
        s_dst[...] = jnp.dot(kaug[pl.ds(k0, t), :], wq[qtile],
                             preferred_element_type=jnp.float32)

    def softmax(qtile, ktile, s_src, p_dst, a_dst, diag):
        s = s_src[...]
        if diag:
            s = s + dbias[...]
            m_new = jnp.max(s, axis=0, keepdims=True)
            p_dst[...] = jnp.exp2(s - m_new).astype(p_dst.dtype)
        else:
            shift = slope2 * ((ktile - qtile) * t).astype(jnp.float32)
            m_old = m_all[qtile]
            m_new = jnp.maximum(m_old, jnp.max(s, axis=0, keepdims=True) + shift)
            p_dst[...] = jnp.exp2(s - (m_new - shift)).astype(p_dst.dtype)
            a_dst[...] = jnp.exp2(m_old - m_new)
        m_all[qtile] = m_new

    def weighted_values(qtile, ktile, p_src, a_src, diag):
        k0 = pl.multiple_of(ktile * t, t)
        pv = jnp.dot(vaug[:, pl.ds(k0, t)], p_src[...], preferred_element_type=jnp.float32)
        acc[qtile] = pv if diag else a_src[...] * acc[qtile] + pv

    def finalize(qtile):
        q0 = pl.multiple_of(qtile * t, t)
        a = acc[qtile]
        o2 = a[:V_HEAD_DIM] * (1.0 / a[V_HEAD_DIM:V_HEAD_DIM + 1])
        o = o2[:, :t] - lam * o2[:, t:]
        o = o * lax.rsqrt(jnp.mean(o * o, axis=0, keepdims=True) + EPS)
        o_ref[0, pl.ds(q0, t), :] = (o.T * gsub_ref[...]).astype(o_ref.dtype)

    bufs = ((s_even, p_even, a_even), (s_odd, p_odd, a_odd))

    spare = (jnp.int32(nq), jnp.int32(0))
    p_odd[...] = jnp.zeros_like(p_odd)
    a_odd[...] = jnp.ones_like(a_odd)
    acc[nq] = jnp.zeros(acc.shape[1:], acc.dtype)

    def run_pairs(first, advance, n_steps, unroll, diag):
        assert n_steps % unroll == 0 and unroll % 2 == 0
        scores(*first, s_even)

        def body(_, state):
            for k in range(unroll):
                (s_cur, p_cur, a_cur), (s_nxt, p_prev, a_prev) = bufs[k % 2], bufs[(k + 1) % 2]
                q, j, qp, jp = state
                qn, jn = advance(q, j)
                qn, jn = jnp.minimum(qn, nq - 1), jnp.minimum(jn, nq - 1)
                scores(qn, jn, s_nxt)
                softmax(q, j, s_cur, p_cur, a_cur, diag)
                weighted_values(qp, jp, p_prev, a_prev, diag)
                state = (qn, jn, q, j)
            return state

        state = lax.fori_loop(0, n_steps // unroll, body, (*first, *spare))
        weighted_values(state[2], state[3], p_odd, a_odd, diag)

    i32 = jnp.int32
    run_pairs((i32(0), i32(0)), lambda q, j: (q + 1, j + 1), nq, DIAG_UNROLL, True)

    def below_diag(q, j):
        wrap = j + 1 == q
        return jnp.where(wrap, q + 1, q), jnp.where(wrap, 0, j + 1)

    run_pairs((i32(1), i32(0)), below_diag, nq * (nq - 1) // 2, PAIR_UNROLL, False)

    per_tile_pair(finalize)


def _attn_call(slopes, lq, gq_col, gsub_row, qt, kn, vt, lambda_init):
    batch, _, seq = qt.shape
    t = ATTN_TILE
    nq = seq // t
    assert nq % DIAG_UNROLL == 0 and (nq * (nq - 1) // 2) % PAIR_UNROLL == 0
    assert t <= 256, "in-tile key offsets must be exact in bf16"
    kernel = functools.partial(_attn_kernel, seq=seq, lambda_init=lambda_init)
    f32, bf = jnp.float32, jnp.bfloat16
    return pl.pallas_call(
        kernel,
        grid=(batch, N_HEADS),
        in_specs=[
            pl.BlockSpec(memory_space=pltpu.SMEM),
            _const_spec(lq.shape),
            _const_spec(gq_col.shape),
            _const_spec(gsub_row.shape),
            pl.BlockSpec((1, 2 * HEAD_DIM, seq), lambda b, h: (b, h, 0)),
            pl.BlockSpec((1, seq, 2 * HEAD_DIM), lambda b, h: (b, 0, h)),
            pl.BlockSpec((1, V_HEAD_DIM, seq), lambda b, h: (b, h, 0)),
        ],
        out_specs=pl.BlockSpec((1, seq, V_HEAD_DIM), lambda b, h: (b, 0, h)),
        out_shape=jax.ShapeDtypeStruct((batch, seq, V_DIM), bf),
        scratch_shapes=[
            pltpu.VMEM((seq, K_AUG), bf),
            pltpu.VMEM((V_AUG, seq), bf),
            pltpu.VMEM((nq, K_AUG, 2 * t), bf),
            pltpu.VMEM((nq + 1, V_AUG, 2 * t), f32),
            pltpu.VMEM((nq, 1, 2 * t), f32),
            pltpu.VMEM((t, 2 * t), f32),
            pltpu.VMEM((t, 2 * t), f32),
            pltpu.VMEM((t, 2 * t), bf),
            pltpu.VMEM((t, 2 * t), bf),
            pltpu.VMEM((1, 2 * t), f32),
            pltpu.VMEM((1, 2 * t), f32),
            pltpu.VMEM((t, 2 * t), f32),
        ],
        compiler_params=pltpu.CompilerParams(
            dimension_semantics=("arbitrary", "arbitrary"), vmem_limit_bytes=VMEM_LIMIT_BYTES),
        name="attn",
    )(slopes, lq, gq_col, gsub_row, qt, kn, vt)


def _mix_kernel(x_ref, u_ref, halo_ref, ya_ref, gate_ref, wpg_ref, ps_ref, wbp_ref, wba_ref,
                wout_ref, gffn_ref, wup_ref, wdown_ref, o_ref, *, tiles_per_seq):
    tm = TOKEN_TILE
    i = pl.program_id(0)
    tile_in_seq = i % tiles_per_seq

    halo = jnp.where(tile_in_seq == 0, 0.0, halo_ref[...].astype(jnp.float32))
    u = u_ref[...].astype(jnp.float32)
    ext = jnp.concatenate([halo, u], axis=0)
    pos = tile_in_seq * tm + lax.broadcasted_iota(jnp.int32, (tm, POOL_GROUP_DIM), 0)
    win = ext
    ys = []
    for g, w in enumerate(POOL_WINDOWS):
        win = win[:, POOL_GROUP_DIM * (1 if g else 0):]
        win = win + pltpu.roll(win, w // 2, 0)
        cnt = jnp.minimum(pos + 1, w).astype(jnp.float32)
        ug = u[:, g * POOL_GROUP_DIM:(g + 1) * POOL_GROUP_DIM]
        mixed = win[POOL_HALO:, :POOL_GROUP_DIM] / cnt - ug
        ys.append(jnp.dot(mixed.astype(jnp.bfloat16), wpg_ref[g],
                          preferred_element_type=jnp.float32))
    y_pool = (jnp.concatenate(ys, axis=1) * ps_ref[...]).astype(jnp.bfloat16)

    bp = jnp.dot(y_pool, wbp_ref[...], preferred_element_type=jnp.float32)
    ba = jnp.dot(ya_ref[...], wba_ref[...], preferred_element_type=jnp.float32)
    gates = gate_ref[...].astype(jnp.float32)
    merged = gates[:, :D_MODEL] * bp + gates[:, D_MODEL:] * ba
    x1 = x_ref[...] + jnp.dot(merged.astype(jnp.bfloat16), wout_ref[...],
                              preferred_element_type=jnp.float32)

    ms = jnp.mean(x1 * x1, axis=-1, keepdims=True)
    h2 = (x1 * lax.rsqrt(ms + EPS) * gffn_ref[...]).astype(jnp.bfloat16)
    y = x1
    for cidx in range(D_FF // FF_CHUNK):
        sl = slice(cidx * FF_CHUNK, (cidx + 1) * FF_CHUNK)
        up = jnp.maximum(jnp.dot(h2, wup_ref[:, sl], preferred_element_type=jnp.float32), 0.0)
        y = y + jnp.dot((up * up).astype(jnp.bfloat16), wdown_ref[sl, :],
                        preferred_element_type=jnp.float32)
    o_ref[...] = y


def _mix_call(x2d, u, y_attn, gates, w_pg, pool_scale, w_bp, w_ba, w_out, g_ffn, w_up, w_down,
              seq):
    tokens = x2d.shape[0]
    tm = TOKEN_TILE
    row = lambda i: (i, 0)
    halo_blocks = tm // POOL_HALO
    kernel = functools.partial(_mix_kernel, tiles_per_seq=seq // tm)
    return pl.pallas_call(
        kernel,
        grid=(tokens // tm,),
        in_specs=[
            pl.BlockSpec((tm, D_MODEL), row),
            pl.BlockSpec((tm, POOL_DIM), row),
            pl.BlockSpec((POOL_HALO, POOL_DIM),
                         lambda i: (jnp.maximum(i * halo_blocks - 1, 0), 0)),
            pl.BlockSpec((tm, V_DIM), row),
            pl.BlockSpec((tm, 2 * D_MODEL), row),
            _const_spec(w_pg.shape),
            _const_spec((1, POOL_DIM)),
            _const_spec(w_bp.shape),
            _const_spec(w_ba.shape),
            _const_spec(w_out.shape),
            _const_spec((1, D_MODEL)),
            _const_spec(w_up.shape),
            _const_spec(w_down.shape),
        ],
        out_specs=pl.BlockSpec((tm, D_MODEL), row),
        out_shape=jax.ShapeDtypeStruct((tokens, D_MODEL), jnp.float32),
        compiler_params=pltpu.CompilerParams(
            dimension_semantics=("arbitrary",), vmem_limit_bytes=VMEM_LIMIT_BYTES),
        name="mix",
    )(x2d, u, u, y_attn, gates, w_pg, pool_scale, w_bp, w_ba, w_out, g_ffn, w_up, w_down)


def kernel(x, g_mix, w_in, w_pool_grp, pool_scale, g_q, g_k, lambda_qk, g_sub, w_branch_pool,
           w_branch_attn, w_gate, b_gate, w_out, g_ffn, w_up, w_down):
    batch, seq, d_model = x.shape
    depth = w_in.shape[0]
    assert d_model == D_MODEL and seq % TOKEN_TILE == 0 and seq % ATTN_TILE == 0
    assert TOKEN_TILE % POOL_HALO == 0 and POOL_HALO >= max(POOL_WINDOWS) - 1
    bf = jnp.bfloat16
    f32 = jnp.float32

    slopes = jnp.asarray([2.0 ** (-8.0 * (i + 1) / N_HEADS) for i in range(N_HEADS)], f32)
    grp = jnp.arange(MXU_COLS_V7X) // HEAD_DIM
    e_blk = (grp[:, None] == grp[None, :]).astype(bf)

    x2d = x.reshape(batch * seq, d_model)
    for l in range(depth):
        lambda_init = 0.8 - 0.6 * math.exp(-0.3 * l)
        k_lo, v_lo = POOL_DIM + QK_DIM, POOL_DIM + 2 * QK_DIM
        w_pk = jnp.concatenate([w_in[l][:, :POOL_DIM], w_in[l][:, k_lo:v_lo]], axis=1).astype(bf)
        w_qvt = jnp.concatenate([w_in[l][:, POOL_DIM:k_lo], w_in[l][:, v_lo:]], axis=1).T.astype(bf)
        gk_row = jnp.tile(g_k[l].astype(f32), 2 * N_HEADS).reshape(1, QK_DIM)
        gq_col = (g_q[l].astype(f32) * (HEAD_DIM ** -0.5 * LOG2E)).reshape(HEAD_DIM, 1)
        gsub_row = (g_sub[l].astype(f32) * (1.0 - lambda_init)).reshape(1, V_HEAD_DIM)

        u, kn, qt, vt, gates = _proj_call(
            x2d, g_mix[l].reshape(1, d_model).astype(f32), w_pk, w_qvt, w_gate[l].astype(bf),
            b_gate[l].reshape(1, 2 * d_model).astype(f32), gk_row, e_blk, batch, seq)
        y_attn = _attn_call(slopes, lambda_qk[l].astype(f32), gq_col, gsub_row, qt,
                            kn.reshape(batch, seq, QK_DIM), vt, lambda_init)
        x2d = _mix_call(
            x2d, u, y_attn.reshape(batch * seq, V_DIM), gates, w_pool_grp[l].astype(bf),
            pool_scale[l].reshape(1, POOL_DIM).astype(f32), w_branch_pool[l].astype(bf),
            w_branch_attn[l].astype(bf), w_out[l].astype(bf),
            g_ffn[l].reshape(1, d_model).astype(f32), w_up[l].astype(bf), w_down[l].astype(bf), seq)
    return x2d.reshape(batch, seq, d_model)
```

```python
import functools
import math

import jax
import jax.numpy as jnp
from jax import lax
from jax.experimental import pallas as pl
from jax.experimental.pallas import tpu as pltpu

D_MODEL = 1024
CHUNK = 64
POOL_WINDOWS = (2, 4, 8, 16)
N_POOL_GROUPS = len(POOL_WINDOWS)
POOL_GROUP_DIM = 128
POOL_DIM = N_POOL_GROUPS * POOL_GROUP_DIM
N_HEADS = 8
HEAD_DIM = 64
V_HEAD_DIM = 2 * HEAD_DIM
QK_DIM = N_HEADS * 2 * HEAD_DIM
V_DIM = N_HEADS * V_HEAD_DIM
D_FF = 4 * D_MODEL
EPS = 1e-6
LOG2E = math.log2(math.e)

MXU_COLS_V7X = 256
BF16_ROWS_V7X = 16
VMEM_LIMIT_BYTES = 56 * 1024 * 1024

TOKEN_TILE = 512
ATTN_TILE = 256
FAST_UNROLL = 8
DIAG_UNROLL = 4
PAIR_UNROLL = 8
FAST_BOUND_MAX = 48.0
BOUND_MARGIN = 1.02
POOL_HALO = 16
FF_CHUNK = 1024
MASK_VALUE = -1e30

K_AUG = 2 * HEAD_DIM + MXU_COLS_V7X // 2
N_PIECES = 3
V_AUG = V_HEAD_DIM + BF16_ROWS_V7X


def _const_spec(shape):
    nd = len(shape)
    return pl.BlockSpec(shape, lambda *_: (0,) * nd, pipeline_mode=pl.Buffered(1))


def _proj_kernel(x_ref, g_ref, wpk_ref, wqvt_ref, wg_ref, bg_ref, gk_ref, e_ref,
                 u_ref, kn_ref, qt_ref, vt_ref, gate_ref):
    xf = x_ref[...]
    ms = jnp.mean(xf * xf, axis=-1, keepdims=True)
    h = (xf * lax.rsqrt(ms + EPS) * g_ref[...]).astype(jnp.bfloat16)

    zpk = jnp.dot(h, wpk_ref[...], preferred_element_type=jnp.float32)
    u_ref[...] = zpk[:, :POOL_DIM].astype(u_ref.dtype)

    zk = zpk[:, POOL_DIM:]
    sq = zk * zk
    sq_hi = sq.astype(jnp.bfloat16)
    sq_lo = (sq - sq_hi.astype(jnp.float32)).astype(jnp.bfloat16)
    e = e_ref[...]
    parts = []
    for c in range(QK_DIM // MXU_COLS_V7X):
        sl = slice(c * MXU_COLS_V7X, (c + 1) * MXU_COLS_V7X)
        parts.append(jnp.dot(sq_hi[:, sl], e, preferred_element_type=jnp.float32)
                     + jnp.dot(sq_lo[:, sl], e, preferred_element_type=jnp.float32))
    ssq = jnp.concatenate(parts, axis=1)
    kn = zk * lax.rsqrt(ssq * (1.0 / HEAD_DIM) + EPS) * gk_ref[...]
    kn_ref[...] = kn.astype(kn_ref.dtype)

    qvt = lax.dot_general(wqvt_ref[...], h, (((1,), (1,)), ((), ())),
                          preferred_element_type=jnp.float32)
    qt_ref[0] = qvt[:QK_DIM].astype(qt_ref.dtype)
    vt_ref[0] = qvt[QK_DIM:].astype(vt_ref.dtype)

    zg = jnp.dot(h, wg_ref[...], preferred_element_type=jnp.float32) + bg_ref[...]
    gate_ref[...] = (1.0 / (1.0 + jnp.exp(-zg))).astype(gate_ref.dtype)


def _proj_call(x2d, g_mix, w_pk, w_qvt, w_gate, b_gate, gk_row, e_blk, batch, seq):
    tokens = x2d.shape[0]
    tm = TOKEN_TILE
    nst = seq // tm
    bf = jnp.bfloat16
    row = lambda i: (i, 0)
    tr = lambda i: (i // nst, 0, i % nst)
    return pl.pallas_call(
        _proj_kernel,
        grid=(tokens // tm,),
        in_specs=[
            pl.BlockSpec((tm, D_MODEL), row),
            _const_spec((1, D_MODEL)),
            _const_spec(w_pk.shape),
            _const_spec(w_qvt.shape),
            _const_spec(w_gate.shape),
            _const_spec((1, 2 * D_MODEL)),
            _const_spec((1, QK_DIM)),
            _const_spec(e_blk.shape),
        ],
        out_specs=[
            pl.BlockSpec((tm, POOL_DIM), row),
            pl.BlockSpec((tm, QK_DIM), row),
            pl.BlockSpec((1, QK_DIM, tm), tr),
            pl.BlockSpec((1, V_DIM, tm), tr),
            pl.BlockSpec((tm, 2 * D_MODEL), row),
        ],
        out_shape=[
            jax.ShapeDtypeStruct((tokens, POOL_DIM), bf),
            jax.ShapeDtypeStruct((tokens, QK_DIM), bf),
            jax.ShapeDtypeStruct((batch, QK_DIM, seq), bf),
            jax.ShapeDtypeStruct((batch, V_DIM, seq), bf),
            jax.ShapeDtypeStruct((tokens, 2 * D_MODEL), bf),
        ],
        compiler_params=pltpu.CompilerParams(
            dimension_semantics=("arbitrary",), vmem_limit_bytes=VMEM_LIMIT_BYTES),
        name="proj",
    )(x2d, g_mix, w_pk, w_qvt, w_gate, b_gate, gk_row, e_blk)


def _attn_kernel(slopes_ref, fast_ref, bound_ref, lq_ref, gq_ref, gsub_ref, qt_ref, k_ref, vt_ref,
                 o_ref, kaug, vaug, wq, acc, m_all, s_even, s_odd, p_even, p_odd, a_even, a_odd,
                 dbias, *, seq, lambda_init):
    t = ATTN_TILE
    nq = seq // t
    slope2 = slopes_ref[pl.program_id(1)] * LOG2E
    n_extra = K_AUG - 2 * HEAD_DIM

    kaug[:, :2 * HEAD_DIM] = k_ref[0]
    row = lax.broadcasted_iota(jnp.int32, (t, n_extra), 0).astype(jnp.float32)
    lane = lax.broadcasted_iota(jnp.int32, (t, n_extra), 1)
    for i in range(nq):
        cols = jnp.where(lane < N_PIECES, row,
                         jnp.where(lane < 2 * N_PIECES, float(i),
                                   jnp.where(lane < 3 * N_PIECES, 1.0, 0.0)))
        kaug[i * t:(i + 1) * t, 2 * HEAD_DIM:] = cols.astype(kaug.dtype)
    vaug[:V_HEAD_DIM, :] = vt_ref[0]
    vaug[V_HEAD_DIM:, :] = jnp.ones((V_AUG - V_HEAD_DIM, seq), vaug.dtype)

    def pieces(x):
        hi = x.astype(jnp.bfloat16).astype(jnp.float32)
        mid = (x - hi).astype(jnp.bfloat16).astype(jnp.float32)
        return hi, mid, x - hi - mid

    def stack_rows(groups):
        prow = lax.broadcasted_iota(jnp.int32, (BF16_ROWS_V7X, 2 * t), 0)
        out = jnp.zeros((BF16_ROWS_V7X, 2 * t), jnp.float32)
        for idx, rowv in enumerate(groups):
            out = jnp.where(prow == idx, rowv, out)
        return out.astype(wq.dtype)

    slope_pieces = pieces(jnp.full((1, 2 * t), slope2, jnp.float32))

    r = lax.broadcasted_iota(jnp.int32, (t, t), 0)
    c = lax.broadcasted_iota(jnp.int32, (t, t), 1)
    ahead = slope2 * (2.0 * (c - r).astype(jnp.float32))
    one_map = jnp.where((r // CHUNK) <= (c // CHUNK), jnp.where(r <= c, 0.0, ahead), MASK_VALUE)
    dbias[0] = jnp.zeros(dbias.shape[1:], dbias.dtype)
    dbias[1, :, :t] = one_map
    dbias[1, :, t:] = one_map

    lq = lq_ref[...]
    lam = (jnp.exp(jnp.sum(lq[0:1] * lq[1:2], axis=1, keepdims=True))
           - jnp.exp(jnp.sum(lq[2:3] * lq[3:4], axis=1, keepdims=True)) + lambda_init)
    gq = gq_ref[...]

    def build_wq(qtile, position_rows):
        q0 = pl.multiple_of(qtile * t, t)
        qt = qt_ref[0, :, pl.ds(q0, t)].astype(jnp.float32)
        zeros = jnp.zeros((HEAD_DIM, t), jnp.float32)
        for half in range(2):
            qh = qt[half * HEAD_DIM:(half + 1) * HEAD_DIM]
            qh = qh * lax.rsqrt(jnp.mean(qh * qh, axis=0, keepdims=True) + EPS) * gq
            both = jnp.concatenate([zeros, qh] if half else [qh, zeros], axis=1)
            wq[qtile, half * HEAD_DIM:(half + 1) * HEAD_DIM, :] = both.astype(wq.dtype)
        wq[qtile, 2 * HEAD_DIM:2 * HEAD_DIM + BF16_ROWS_V7X, :] = position_rows(qtile)
        wq[qtile, 2 * HEAD_DIM + BF16_ROWS_V7X:, :] = jnp.zeros(
            (n_extra - BF16_ROWS_V7X, 2 * t), wq.dtype)

    def per_tile_pair(fn):
        def body(i, carry):
            fn(2 * i)
            fn(2 * i + 1)
            return carry
        lax.fori_loop(0, nq // 2, body, 0)

    def finalize(qtile):
        q0 = pl.multiple_of(qtile * t, t)
        a = acc[qtile]
        o2 = a[:V_HEAD_DIM] * (1.0 / a[V_HEAD_DIM:V_HEAD_DIM + 1])
        o = o2[:, :t] - lam * o2[:, t:]
        o = o * lax.rsqrt(jnp.mean(o * o, axis=0, keepdims=True) + EPS)
        o_ref[0, pl.ds(q0, t), :] = (o.T * gsub_ref[...]).astype(o_ref.dtype)

    i32 = jnp.int32

    @pl.when(fast_ref[0] == 1)
    def _():
        _attn_fast_sweep(t, nq, slope2, bound_ref[0], slope_pieces, pieces, stack_rows, build_wq,
                         per_tile_pair, kaug, vaug, wq, acc, s_even, s_odd, p_even, p_odd, dbias)

    @pl.when(fast_ref[0] == 0)
    def _():
        _attn_safe_sweep(t, nq, slope2, slope_pieces, stack_rows, build_wq, per_tile_pair, kaug,
                         vaug, wq, acc, m_all, s_even, s_odd, p_even, p_odd, a_even, a_odd, dbias)

    per_tile_pair(finalize)


def _attn_fast_sweep(t, nq, slope2, bound, slope_pieces, pieces, stack_rows, build_wq,
                     per_tile_pair, kaug, vaug, wq, acc, s_even, s_odd, p_even, p_odd, dbias):
    tile_pieces = tuple(p * float(t) for p in slope_pieces)
    col = lax.broadcasted_iota(jnp.int32, (1, 2 * t), 1)
    col = jnp.where(col >= t, col - t, col)

    def position_rows(qtile):
        qpos = (qtile * t + col).astype(jnp.float32)
        return stack_rows(slope_pieces + tile_pieces + pieces(-(slope2 * qpos) - bound))

    per_tile_pair(functools.partial(build_wq, position_rows=position_rows))

    def scores(pair, s_dst):
        k0 = pl.multiple_of(pair[1] * t, t)
        s_dst[...] = jnp.dot(kaug[pl.ds(k0, t), :], wq[pair[0]],
                             preferred_element_type=jnp.float32)

    def probabilities(pair, s_src, p_dst):
        s = s_src[...] + dbias[(pair[1] == pair[0]).astype(jnp.int32)]
        p_dst[...] = jnp.exp2(s).astype(p_dst.dtype)

    def weighted_values(pair, p_src):
        qtile, ktile = pair
        k0 = pl.multiple_of(ktile * t, t)
        pv = jnp.dot(vaug[:, pl.ds(k0, t)], p_src[...], preferred_element_type=jnp.float32)
        acc[qtile] = jnp.where(ktile == 0, 0.0, acc[qtile]) + pv

    def advance(pair):
        q, j = pair
        wrap = j == q
        return jnp.minimum(jnp.where(wrap, q + 1, q), nq - 1), jnp.where(wrap, 0, j + 1)

    n_steps = nq * (nq + 1) // 2
    assert n_steps % FAST_UNROLL == 0 and FAST_UNROLL % 2 == 0
    bufs = ((s_even, p_even), (s_odd, p_odd))

    pair0 = (jnp.int32(0), jnp.int32(0))
    pair1 = advance(pair0)
    scores(pair0, s_even)
    probabilities(pair0, s_even, p_even)
    scores(pair1, s_odd)

    def body(_, state):
        for k in range(FAST_UNROLL):
            (s_a, p_a), (s_b, p_b) = bufs[k % 2], bufs[(k + 1) % 2]
            cur, nxt = state
            nxt2 = advance(nxt)
            scores(nxt2, s_a)
            probabilities(nxt, s_b, p_b)
            weighted_values(cur, p_a)
            state = (nxt, nxt2)
        return state

    lax.fori_loop(0, n_steps // FAST_UNROLL, body, (pair0, pair1))


def _attn_safe_sweep(t, nq, slope2, slope_pieces, stack_rows, build_wq, per_tile_pair, kaug, vaug,
                     wq, acc, m_all, s_even, s_odd, p_even, p_odd, a_even, a_odd, dbias):
    only_offset_rows = stack_rows(slope_pieces)
    per_tile_pair(functools.partial(build_wq, position_rows=lambda qtile: only_offset_rows))

    def scores(qtile, ktile, s_dst):
        k0 = pl.multiple_of(ktile * t, t)
        s_dst[...] = jnp.dot(kaug[pl.ds(k0, t), :], wq[qtile],
                             preferred_element_type=jnp.float32)

    def softmax(qtile, ktile, s_src, p_dst, a_dst, diag):
        s = s_src[...]
        if diag:
            s = s + dbias[1]
            m_new = jnp.max(s, axis=0, keepdims=True)
            p_dst[...] = jnp.exp2(s - m_new).astype(p_dst.dtype)
        else:
            shift = slope2 * ((ktile - qtile) * t).astype(jnp.float32)
            m_old = m_all[qtile]
            m_new = jnp.maximum(m_old, jnp.max(s, axis=0, keepdims=True) + shift)
            p_dst[...] = jnp.exp2(s - (m_new - shift)).astype(p_dst.dtype)
            a_dst[...] = jnp.exp2(m_old - m_new)
        m_all[qtile] = m_new

    def weighted_values(qtile, ktile, p_src, a_src, diag):
        k0 = pl.multiple_of(ktile * t, t)
        pv = jnp.dot(vaug[:, pl.ds(k0, t)], p_src[...], preferred_element_type=jnp.float32)
        acc[qtile] = pv if diag else a_src[...] * acc[qtile] + pv

    bufs = ((s_even, p_even, a_even), (s_odd, p_odd, a_odd))

    spare = (jnp.int32(nq), jnp.int32(0))
    p_odd[...] = jnp.zeros_like(p_odd)
    a_odd[...] = jnp.ones_like(a_odd)
    acc[nq] = jnp.zeros(acc.shape[1:], acc.dtype)

    def run_pairs(first, advance, n_steps, unroll, diag):
        assert n_steps % unroll == 0 and unroll % 2 == 0
        scores(*first, s_even)

        def body(_, state):
            for k in range(unroll):
                (s_cur, p_cur, a_cur), (s_nxt, p_prev, a_prev) = bufs[k % 2], bufs[(k + 1) % 2]
                q, j, qp, jp = state
                qn, jn = advance(q, j)
                qn, jn = jnp.minimum(qn, nq - 1), jnp.minimum(jn, nq - 1)
                scores(qn, jn, s_nxt)
                softmax(q, j, s_cur, p_cur, a_cur, diag)
                weighted_values(qp, jp, p_prev, a_prev, diag)
                state = (qn, jn, q, j)
            return state

        state = lax.fori_loop(0, n_steps // unroll, body, (*first, *spare))
        weighted_values(state[2], state[3], p_odd, a_odd, diag)

    i32 = jnp.int32
    run_pairs((i32(0), i32(0)), lambda q, j: (q + 1, j + 1), nq, DIAG_UNROLL, True)

    def below_diag(q, j):
        wrap = j + 1 == q
        return jnp.where(wrap, q + 1, q), jnp.where(wrap, 0, j + 1)

    run_pairs((i32(1), i32(0)), below_diag, nq * (nq - 1) // 2, PAIR_UNROLL, False)


def _attn_call(slopes, lq, gq_col, gk_max, gsub_row, qt, kn, vt, lambda_init):
    batch, _, seq = qt.shape
    t = ATTN_TILE
    nq = seq // t
    assert nq % 2 == 0 and nq % DIAG_UNROLL == 0 and (nq * (nq - 1) // 2) % PAIR_UNROLL == 0
    assert t <= 256, "in-tile key offsets must be exact in bf16"
    kernel = functools.partial(_attn_kernel, seq=seq, lambda_init=lambda_init)
    f32, bf = jnp.float32, jnp.bfloat16
    bound = (BOUND_MARGIN * HEAD_DIM * jnp.max(jnp.abs(gq_col)) * gk_max).astype(f32).reshape(1)
    fast = (bound <= FAST_BOUND_MAX).astype(jnp.int32)
    return pl.pallas_call(
        kernel,
        grid=(batch, N_HEADS),
        in_specs=[
            pl.BlockSpec(memory_space=pltpu.SMEM),
            pl.BlockSpec(memory_space=pltpu.SMEM),
            pl.BlockSpec(memory_space=pltpu.SMEM),
            _const_spec(lq.shape),
            _const_spec(gq_col.shape),
            _const_spec(gsub_row.shape),
            pl.BlockSpec((1, 2 * HEAD_DIM, seq), lambda b, h: (b, h, 0)),
            pl.BlockSpec((1, seq, 2 * HEAD_DIM), lambda b, h: (b, 0, h)),
            pl.BlockSpec((1, V_HEAD_DIM, seq), lambda b, h: (b, h, 0)),
        ],
        out_specs=pl.BlockSpec((1, seq, V_HEAD_DIM), lambda b, h: (b, 0, h)),
        out_shape=jax.ShapeDtypeStruct((batch, seq, V_DIM), bf),
        scratch_shapes=[
            pltpu.VMEM((seq, K_AUG), bf),
            pltpu.VMEM((V_AUG, seq), bf),
            pltpu.VMEM((nq, K_AUG, 2 * t), bf),
            pltpu.VMEM((nq + 1, V_AUG, 2 * t), f32),
            pltpu.VMEM((nq, 1, 2 * t), f32),
            pltpu.VMEM((t, 2 * t), f32),
            pltpu.VMEM((t, 2 * t), f32),
            pltpu.VMEM((t, 2 * t), bf),
            pltpu.VMEM((t, 2 * t), bf),
            pltpu.VMEM((1, 2 * t), f32),
            pltpu.VMEM((1, 2 * t), f32),
            pltpu.VMEM((2, t, 2 * t), f32),
        ],
        compiler_params=pltpu.CompilerParams(
            dimension_semantics=("arbitrary", "arbitrary"), vmem_limit_bytes=VMEM_LIMIT_BYTES),
        name="attn",
    )(slopes, fast, bound, lq, gq_col, gsub_row, qt, kn, vt)


def _mix_kernel(x_ref, u_ref, halo_ref, ya_ref, gate_ref, wpg_ref, ps_ref, wbp_ref, wba_ref,
                wout_ref, gffn_ref, wup_ref, wdown_ref, o_ref, *, tiles_per_seq):
    tm = TOKEN_TILE
    i = pl.program_id(0)
    tile_in_seq = i % tiles_per_seq

    halo = jnp.where(tile_in_seq == 0, 0.0, halo_ref[...].astype(jnp.float32))
    u = u_ref[...].astype(jnp.float32)
    ext = jnp.concatenate([halo, u], axis=0)
    pos = tile_in_seq * tm + lax.broadcasted_iota(jnp.int32, (tm, POOL_GROUP_DIM), 0)
    win = ext
    ys = []
    for g, w in enumerate(POOL_WINDOWS):
        win = win[:, POOL_GROUP_DIM * (1 if g else 0):]
        win = win + pltpu.roll(win, w // 2, 0)
        cnt = jnp.minimum(pos + 1, w).astype(jnp.float32)
        ug = u[:, g * POOL_GROUP_DIM:(g + 1) * POOL_GROUP_DIM]
        mixed = win[POOL_HALO:, :POOL_GROUP_DIM] / cnt - ug
        ys.append(jnp.dot(mixed.astype(jnp.bfloat16), wpg_ref[g],
                          preferred_element_type=jnp.float32))
    y_pool = (jnp.concatenate(ys, axis=1) * ps_ref[...]).astype(jnp.bfloat16)

    bp = jnp.dot(y_pool, wbp_ref[...], preferred_element_type=jnp.float32)
    ba = jnp.dot(ya_ref[...], wba_ref[...], preferred_element_type=jnp.float32)
    gates = gate_ref[...].astype(jnp.float32)
    merged = gates[:, :D_MODEL] * bp + gates[:, D_MODEL:] * ba
    x1 = x_ref[...] + jnp.dot(merged.astype(jnp.bfloat16), wout_ref[...],
                              preferred_element_type=jnp.float32)

    ms = jnp.mean(x1 * x1, axis=-1, keepdims=True)
    h2 = (x1 * lax.rsqrt(ms + EPS) * gffn_ref[...]).astype(jnp.bfloat16)
    y = x1
    for cidx in range(D_FF // FF_CHUNK):
        sl = slice(cidx * FF_CHUNK, (cidx + 1) * FF_CHUNK)
        up = jnp.maximum(jnp.dot(h2, wup_ref[:, sl], preferred_element_type=jnp.float32), 0.0)
        y = y + jnp.dot((up * up).astype(jnp.bfloat16), wdown_ref[sl, :],
                        preferred_element_type=jnp.float32)
    o_ref[...] = y


def _mix_call(x2d, u, y_attn, gates, w_pg, pool_scale, w_bp, w_ba, w_out, g_ffn, w_up, w_down,
              seq):
    tokens = x2d.shape[0]
    tm = TOKEN_TILE
    row = lambda i: (i, 0)
    halo_blocks = tm // POOL_HALO
    kernel = functools.partial(_mix_kernel, tiles_per_seq=seq // tm)
    return pl.pallas_call(
        kernel,
        grid=(tokens // tm,),
        in_specs=[
            pl.BlockSpec((tm, D_MODEL), row),
            pl.BlockSpec((tm, POOL_DIM), row),
            pl.BlockSpec((POOL_HALO, POOL_DIM),
                         lambda i: (jnp.maximum(i * halo_blocks - 1, 0), 0)),
            pl.BlockSpec((tm, V_DIM), row),
            pl.BlockSpec((tm, 2 * D_MODEL), row),
            _const_spec(w_pg.shape),
            _const_spec((1, POOL_DIM)),
            _const_spec(w_bp.shape),
            _const_spec(w_ba.shape),
            _const_spec(w_out.shape),
            _const_spec((1, D_MODEL)),
            _const_spec(w_up.shape),
            _const_spec(w_down.shape),
        ],
        out_specs=pl.BlockSpec((tm, D_MODEL), row),
        out_shape=jax.ShapeDtypeStruct((tokens, D_MODEL), jnp.float32),
        compiler_params=pltpu.CompilerParams(
            dimension_semantics=("arbitrary",), vmem_limit_bytes=VMEM_LIMIT_BYTES),
        name="mix",
    )(x2d, u, u, y_attn, gates, w_pg, pool_scale, w_bp, w_ba, w_out, g_ffn, w_up, w_down)


def kernel(x, g_mix, w_in, w_pool_grp, pool_scale, g_q, g_k, lambda_qk, g_sub, w_branch_pool,
           w_branch_attn, w_gate, b_gate, w_out, g_ffn, w_up, w_down):
    batch, seq, d_model = x.shape
    depth = w_in.shape[0]
    assert d_model == D_MODEL and seq % TOKEN_TILE == 0 and seq % ATTN_TILE == 0
    assert TOKEN_TILE % POOL_HALO == 0 and POOL_HALO >= max(POOL_WINDOWS) - 1
    bf = jnp.bfloat16
    f32 = jnp.float32

    slopes = jnp.asarray([2.0 ** (-8.0 * (i + 1) / N_HEADS) for i in range(N_HEADS)], f32)
    grp = jnp.arange(MXU_COLS_V7X) // HEAD_DIM
    e_blk = (grp[:, None] == grp[None, :]).astype(bf)

    x2d = x.reshape(batch * seq, d_model)
    for l in range(depth):
        lambda_init = 0.8 - 0.6 * math.exp(-0.3 * l)
        k_lo, v_lo = POOL_DIM + QK_DIM, POOL_DIM + 2 * QK_DIM
        w_pk = jnp.concatenate([w_in[l][:, :POOL_DIM], w_in[l][:, k_lo:v_lo]], axis=1).astype(bf)
        w_qvt = jnp.concatenate([w_in[l][:, POOL_DIM:k_lo], w_in[l][:, v_lo:]], axis=1).T.astype(bf)
        gk_row = jnp.tile(g_k[l].astype(f32), 2 * N_HEADS).reshape(1, QK_DIM)
        gq_col = (g_q[l].astype(f32) * (HEAD_DIM ** -0.5 * LOG2E)).reshape(HEAD_DIM, 1)
        gsub_row = (g_sub[l].astype(f32) * (1.0 - lambda_init)).reshape(1, V_HEAD_DIM)

        u, kn, qt, vt, gates = _proj_call(
            x2d, g_mix[l].reshape(1, d_model).astype(f32), w_pk, w_qvt, w_gate[l].astype(bf),
            b_gate[l].reshape(1, 2 * d_model).astype(f32), gk_row, e_blk, batch, seq)
        y_attn = _attn_call(slopes, lambda_qk[l].astype(f32), gq_col,
                            jnp.max(jnp.abs(g_k[l].astype(f32))), gsub_row, qt,
                            kn.reshape(batch, seq, QK_DIM), vt, lambda_init)
        x2d = _mix_call(
            x2d, u, y_attn.reshape(batch * seq, V_DIM), gates, w_pool_grp[l].astype(bf),
            pool_scale[l].reshape(1, POOL_DIM).astype(f32), w_branch_pool[l].astype(bf),
            w_branch_attn[l].astype(bf), w_out[l].astype(bf),
            g_ffn[l].reshape(1, d_model).astype(f32), w_up[l].astype(bf), w_down[l].astype(bf), seq)
    return x2d.reshape(batch, seq, d_model)
```

```python
import functools
import math

import jax
import jax.numpy as jnp
from jax import lax
from jax.experimental import pallas as pl
from jax.experimental.pallas import tpu as pltpu

D_MODEL = 1024
CHUNK = 64
POOL_WINDOWS = (2, 4, 8, 16)
N_POOL_GROUPS = len(POOL_WINDOWS)
POOL_GROUP_DIM = 128
POOL_DIM = N_POOL_GROUPS * POOL_GROUP_DIM
N_HEADS = 8
HEAD_DIM = 64
V_HEAD_DIM = 2 * HEAD_DIM
QK_DIM = N_HEADS * 2 * HEAD_DIM
V_DIM = N_HEADS * V_HEAD_DIM
D_FF = 4 * D_MODEL
EPS = 1e-6
LOG2E = math.log2(math.e)

MXU_COLS_V7X = 256
BF16_ROWS_V7X = 16
VMEM_LIMIT_BYTES = 56 * 1024 * 1024

TOKEN_TILE = 512
ATTN_TILE = 256
FAST_UNROLL = 8
FAST_DIAG_UNROLL = 4
DIAG_UNROLL = 4
PAIR_UNROLL = 8
FAST_BOUND_MAX = 48.0
BOUND_MARGIN = 1.02
POOL_HALO = 16
FF_CHUNK = 1024
MASK_VALUE = -1e30

K_AUG = 2 * HEAD_DIM + MXU_COLS_V7X // 2
N_PIECES = 3
V_AUG = V_HEAD_DIM + BF16_ROWS_V7X


def _const_spec(shape):
    nd = len(shape)
    return pl.BlockSpec(shape, lambda *_: (0,) * nd, pipeline_mode=pl.Buffered(1))


def _proj_kernel(x_ref, g_ref, wpk_ref, wqvt_ref, wg_ref, bg_ref, gk_ref, e_ref,
                 u_ref, kn_ref, qt_ref, vt_ref, gate_ref):
    xf = x_ref[...]
    ms = jnp.mean(xf * xf, axis=-1, keepdims=True)
    h = (xf * lax.rsqrt(ms + EPS) * g_ref[...]).astype(jnp.bfloat16)

    zpk = jnp.dot(h, wpk_ref[...], preferred_element_type=jnp.float32)
    u_ref[...] = zpk[:, :POOL_DIM].astype(u_ref.dtype)

    zk = zpk[:, POOL_DIM:]
    sq = zk * zk
    sq_hi = sq.astype(jnp.bfloat16)
    sq_lo = (sq - sq_hi.astype(jnp.float32)).astype(jnp.bfloat16)
    e = e_ref[...]
    parts = []
    for c in range(QK_DIM // MXU_COLS_V7X):
        sl = slice(c * MXU_COLS_V7X, (c + 1) * MXU_COLS_V7X)
        parts.append(jnp.dot(sq_hi[:, sl], e, preferred_element_type=jnp.float32)
                     + jnp.dot(sq_lo[:, sl], e, preferred_element_type=jnp.float32))
    ssq = jnp.concatenate(parts, axis=1)
    kn = zk * lax.rsqrt(ssq * (1.0 / HEAD_DIM) + EPS) * gk_ref[...]
    kn_ref[...] = kn.astype(kn_ref.dtype)

    qvt = lax.dot_general(wqvt_ref[...], h, (((1,), (1,)), ((), ())),
                          preferred_element_type=jnp.float32)
    qt_ref[0] = qvt[:QK_DIM].astype(qt_ref.dtype)
    vt_ref[0] = qvt[QK_DIM:].astype(vt_ref.dtype)

    zg = jnp.dot(h, wg_ref[...], preferred_element_type=jnp.float32) + bg_ref[...]
    gate_ref[...] = (1.0 / (1.0 + jnp.exp(-zg))).astype(gate_ref.dtype)


def _proj_call(x2d, g_mix, w_pk, w_qvt, w_gate, b_gate, gk_row, e_blk, batch, seq):
    tokens = x2d.shape[0]
    tm = TOKEN_TILE
    nst = seq // tm
    bf = jnp.bfloat16
    row = lambda i: (i, 0)
    tr = lambda i: (i // nst, 0, i % nst)
    return pl.pallas_call(
        _proj_kernel,
        grid=(tokens // tm,),
        in_specs=[
            pl.BlockSpec((tm, D_MODEL), row),
            _const_spec((1, D_MODEL)),
            _const_spec(w_pk.shape),
            _const_spec(w_qvt.shape),
            _const_spec(w_gate.shape),
            _const_spec((1, 2 * D_MODEL)),
            _const_spec((1, QK_DIM)),
            _const_spec(e_blk.shape),
        ],
        out_specs=[
            pl.BlockSpec((tm, POOL_DIM), row),
            pl.BlockSpec((tm, QK_DIM), row),
            pl.BlockSpec((1, QK_DIM, tm), tr),
            pl.BlockSpec((1, V_DIM, tm), tr),
            pl.BlockSpec((tm, 2 * D_MODEL), row),
        ],
        out_shape=[
            jax.ShapeDtypeStruct((tokens, POOL_DIM), bf),
            jax.ShapeDtypeStruct((tokens, QK_DIM), bf),
            jax.ShapeDtypeStruct((batch, QK_DIM, seq), bf),
            jax.ShapeDtypeStruct((batch, V_DIM, seq), bf),
            jax.ShapeDtypeStruct((tokens, 2 * D_MODEL), bf),
        ],
        compiler_params=pltpu.CompilerParams(
            dimension_semantics=("arbitrary",), vmem_limit_bytes=VMEM_LIMIT_BYTES),
        name="proj",
    )(x2d, g_mix, w_pk, w_qvt, w_gate, b_gate, gk_row, e_blk)


def _attn_kernel(slopes_ref, fast_ref, bound_ref, lq_ref, gq_ref, gsub_ref, qt_ref, k_ref, vt_ref,
                 o_ref, kaug, vaug, wq, acc, m_all, s_even, s_odd, p_even, p_odd, a_even, a_odd,
                 dbias, *, seq, lambda_init):
    t = ATTN_TILE
    nq = seq // t
    slope2 = slopes_ref[pl.program_id(1)] * LOG2E
    n_extra = K_AUG - 2 * HEAD_DIM

    kaug[:, :2 * HEAD_DIM] = k_ref[0]
    row = lax.broadcasted_iota(jnp.int32, (t, n_extra), 0).astype(jnp.float32)
    lane = lax.broadcasted_iota(jnp.int32, (t, n_extra), 1)
    for i in range(nq):
        cols = jnp.where(lane < N_PIECES, row,
                         jnp.where(lane < 2 * N_PIECES, float(i),
                                   jnp.where(lane < 3 * N_PIECES, 1.0, 0.0)))
        kaug[i * t:(i + 1) * t, 2 * HEAD_DIM:] = cols.astype(kaug.dtype)
    vaug[:V_HEAD_DIM, :] = vt_ref[0]
    vaug[V_HEAD_DIM:, :] = jnp.ones((V_AUG - V_HEAD_DIM, seq), vaug.dtype)

    def pieces(x):
        hi = x.astype(jnp.bfloat16).astype(jnp.float32)
        mid = (x - hi).astype(jnp.bfloat16).astype(jnp.float32)
        return hi, mid, x - hi - mid

    def stack_rows(groups):
        prow = lax.broadcasted_iota(jnp.int32, (BF16_ROWS_V7X, 2 * t), 0)
        out = jnp.zeros((BF16_ROWS_V7X, 2 * t), jnp.float32)
        for idx, rowv in enumerate(groups):
            out = jnp.where(prow == idx, rowv, out)
        return out.astype(wq.dtype)

    slope_pieces = pieces(jnp.full((1, 2 * t), slope2, jnp.float32))

    r = lax.broadcasted_iota(jnp.int32, (t, t), 0)
    c = lax.broadcasted_iota(jnp.int32, (t, t), 1)
    ahead = slope2 * (2.0 * (c - r).astype(jnp.float32))
    one_map = jnp.where((r // CHUNK) <= (c // CHUNK), jnp.where(r <= c, 0.0, ahead), MASK_VALUE)
    dbias[:, :t] = one_map
    dbias[:, t:] = one_map

    lq = lq_ref[...]
    lam = (jnp.exp(jnp.sum(lq[0:1] * lq[1:2], axis=1, keepdims=True))
           - jnp.exp(jnp.sum(lq[2:3] * lq[3:4], axis=1, keepdims=True)) + lambda_init)
    gq = gq_ref[...]

    def build_wq(qtile, position_rows):
        q0 = pl.multiple_of(qtile * t, t)
        qt = qt_ref[0, :, pl.ds(q0, t)].astype(jnp.float32)
        zeros = jnp.zeros((HEAD_DIM, t), jnp.float32)
        for half in range(2):
            qh = qt[half * HEAD_DIM:(half + 1) * HEAD_DIM]
            qh = qh * lax.rsqrt(jnp.mean(qh * qh, axis=0, keepdims=True) + EPS) * gq
            both = jnp.concatenate([zeros, qh] if half else [qh, zeros], axis=1)
            wq[qtile, half * HEAD_DIM:(half + 1) * HEAD_DIM, :] = both.astype(wq.dtype)
        wq[qtile, 2 * HEAD_DIM:2 * HEAD_DIM + BF16_ROWS_V7X, :] = position_rows(qtile)
        wq[qtile, 2 * HEAD_DIM + BF16_ROWS_V7X:, :] = jnp.zeros(
            (n_extra - BF16_ROWS_V7X, 2 * t), wq.dtype)

    def per_tile_pair(fn):
        def body(i, carry):
            fn(2 * i)
            fn(2 * i + 1)
            return carry
        lax.fori_loop(0, nq // 2, body, 0)

    def finalize(qtile):
        q0 = pl.multiple_of(qtile * t, t)
        a = acc[qtile]
        o2 = a[:V_HEAD_DIM] * (1.0 / a[V_HEAD_DIM:V_HEAD_DIM + 1])
        o = o2[:, :t] - lam * o2[:, t:]
        o = o * lax.rsqrt(jnp.mean(o * o, axis=0, keepdims=True) + EPS)
        o_ref[0, pl.ds(q0, t), :] = (o.T * gsub_ref[...]).astype(o_ref.dtype)

    i32 = jnp.int32

    @pl.when(fast_ref[0] == 1)
    def _():
        _attn_fast_sweep(t, nq, slope2, bound_ref[0], slope_pieces, pieces, stack_rows, build_wq,
                         per_tile_pair, kaug, vaug, wq, acc, s_even, s_odd, p_even, p_odd, dbias)

    @pl.when(fast_ref[0] == 0)
    def _():
        _attn_safe_sweep(t, nq, slope2, slope_pieces, stack_rows, build_wq, per_tile_pair, kaug,
                         vaug, wq, acc, m_all, s_even, s_odd, p_even, p_odd, a_even, a_odd, dbias)

    per_tile_pair(finalize)


def _attn_fast_sweep(t, nq, slope2, bound, slope_pieces, pieces, stack_rows, build_wq,
                     per_tile_pair, kaug, vaug, wq, acc, s_even, s_odd, p_even, p_odd, dbias):
    tile_pieces = tuple(p * float(t) for p in slope_pieces)
    col = lax.broadcasted_iota(jnp.int32, (1, 2 * t), 1)
    col = jnp.where(col >= t, col - t, col)

    def position_rows(qtile):
        qpos = (qtile * t + col).astype(jnp.float32)
        return stack_rows(slope_pieces + tile_pieces + pieces(-(slope2 * qpos) - bound))

    per_tile_pair(functools.partial(build_wq, position_rows=position_rows))

    def scores(pair, s_dst):
        k0 = pl.multiple_of(pair[1] * t, t)
        s_dst[...] = jnp.dot(kaug[pl.ds(k0, t), :], wq[pair[0]],
                             preferred_element_type=jnp.float32)

    def probabilities(s_src, p_dst, diag):
        s = s_src[...]
        if diag:
            s = s + dbias[...]
        p_dst[...] = jnp.exp2(s).astype(p_dst.dtype)

    def weighted_values(pair, p_src, diag):
        qtile, ktile = pair
        k0 = pl.multiple_of(ktile * t, t)
        pv = jnp.dot(vaug[:, pl.ds(k0, t)], p_src[...], preferred_element_type=jnp.float32)
        acc[qtile] = pv if diag else acc[qtile] + pv

    bufs = ((s_even, p_even), (s_odd, p_odd))

    def sweep(first, advance, n_steps, unroll, diag):
        assert n_steps % unroll == 0 and unroll % 2 == 0

        def clamped(pair):
            q, j = advance(pair)
            return jnp.minimum(q, nq - 1), jnp.minimum(j, nq - 1)

        second = clamped(first)
        scores(first, s_even)
        probabilities(s_even, p_even, diag)
        scores(second, s_odd)

        def body(_, state):
            for k in range(unroll):
                (s_a, p_a), (s_b, p_b) = bufs[k % 2], bufs[(k + 1) % 2]
                cur, nxt = state
                nxt2 = clamped(nxt)
                scores(nxt2, s_a)
                probabilities(s_b, p_b, diag)
                weighted_values(cur, p_a, diag)
                state = (nxt, nxt2)
            return state

        lax.fori_loop(0, n_steps // unroll, body, (first, second))

    i32 = jnp.int32
    sweep((i32(0), i32(0)), lambda pair: (pair[0] + 1, pair[1] + 1), nq, FAST_DIAG_UNROLL, True)

    def below_diag(pair):
        q, j = pair
        wrap = j + 1 == q
        return jnp.where(wrap, q + 1, q), jnp.where(wrap, 0, j + 1)

    sweep((i32(1), i32(0)), below_diag, nq * (nq - 1) // 2, FAST_UNROLL, False)


def _attn_safe_sweep(t, nq, slope2, slope_pieces, stack_rows, build_wq, per_tile_pair, kaug, vaug,
                     wq, acc, m_all, s_even, s_odd, p_even, p_odd, a_even, a_odd, dbias):
    only_offset_rows = stack_rows(slope_pieces)
    per_tile_pair(functools.partial(build_wq, position_rows=lambda qtile: only_offset_rows))

    def scores(qtile, ktile, s_dst):
        k0 = pl.multiple_of(ktile * t, t)
        s_dst[...] = jnp.dot(kaug[pl.ds(k0, t), :], wq[qtile],
                             preferred_element_type=jnp.float32)

    def softmax(qtile, ktile, s_src, p_dst, a_dst, diag):
        s = s_src[...]
        if diag:
            s = s + dbias[...]
            m_new = jnp.max(s, axis=0, keepdims=True)
            p_dst[...] = jnp.exp2(s - m_new).astype(p_dst.dtype)
        else:
            shift = slope2 * ((ktile - qtile) * t).astype(jnp.float32)
            m_old = m_all[qtile]
            m_new = jnp.maximum(m_old, jnp.max(s, axis=0, keepdims=True) + shift)
            p_dst[...] = jnp.exp2(s - (m_new - shift)).astype(p_dst.dtype)
            a_dst[...] = jnp.exp2(m_old - m_new)
        m_all[qtile] = m_new

    def weighted_values(qtile, ktile, p_src, a_src, diag):
        k0 = pl.multiple_of(ktile * t, t)
        pv = jnp.dot(vaug[:, pl.ds(k0, t)], p_src[...], preferred_element_type=jnp.float32)
        acc[qtile] = pv if diag else a_src[...] * acc[qtile] + pv

    bufs = ((s_even, p_even, a_even), (s_odd, p_odd, a_odd))

    spare = (jnp.int32(nq), jnp.int32(0))
    p_odd[...] = jnp.zeros_like(p_odd)
    a_odd[...] = jnp.ones_like(a_odd)
    acc[nq] = jnp.zeros(acc.shape[1:], acc.dtype)

    def run_pairs(first, advance, n_steps, unroll, diag):
        assert n_steps % unroll == 0 and unroll % 2 == 0
        scores(*first, s_even)

        def body(_, state):
            for k in range(unroll):
                (s_cur, p_cur, a_cur), (s_nxt, p_prev, a_prev) = bufs[k % 2], bufs[(k + 1) % 2]
                q, j, qp, jp = state
                qn, jn = advance(q, j)
                qn, jn = jnp.minimum(qn, nq - 1), jnp.minimum(jn, nq - 1)
                scores(qn, jn, s_nxt)
                softmax(q, j, s_cur, p_cur, a_cur, diag)
                weighted_values(qp, jp, p_prev, a_prev, diag)
                state = (qn, jn, q, j)
            return state

        state = lax.fori_loop(0, n_steps // unroll, body, (*first, *spare))
        weighted_values(state[2], state[3], p_odd, a_odd, diag)

    i32 = jnp.int32
    run_pairs((i32(0), i32(0)), lambda q, j: (q + 1, j + 1), nq, DIAG_UNROLL, True)

    def below_diag(q, j):
        wrap = j + 1 == q
        return jnp.where(wrap, q + 1, q), jnp.where(wrap, 0, j + 1)

    run_pairs((i32(1), i32(0)), below_diag, nq * (nq - 1) // 2, PAIR_UNROLL, False)


def _attn_call(slopes, lq, gq_col, gk_max, gsub_row, qt, kn, vt, lambda_init):
    batch, _, seq = qt.shape
    t = ATTN_TILE
    nq = seq // t
    assert nq % 2 == 0 and nq % DIAG_UNROLL == 0 and (nq * (nq - 1) // 2) % PAIR_UNROLL == 0
    assert nq % FAST_DIAG_UNROLL == 0 and (nq * (nq - 1) // 2) % FAST_UNROLL == 0
    assert t <= 256, "in-tile key offsets must be exact in bf16"
    kernel = functools.partial(_attn_kernel, seq=seq, lambda_init=lambda_init)
    f32, bf = jnp.float32, jnp.bfloat16
    bound = (BOUND_MARGIN * HEAD_DIM * jnp.max(jnp.abs(gq_col)) * gk_max).astype(f32).reshape(1)
    fast = (bound <= FAST_BOUND_MAX).astype(jnp.int32)
    return pl.pallas_call(
        kernel,
        grid=(batch, N_HEADS),
        in_specs=[
            pl.BlockSpec(memory_space=pltpu.SMEM),
            pl.BlockSpec(memory_space=pltpu.SMEM),
            pl.BlockSpec(memory_space=pltpu.SMEM),
            _const_spec(lq.shape),
            _const_spec(gq_col.shape),
            _const_spec(gsub_row.shape),
            pl.BlockSpec((1, 2 * HEAD_DIM, seq), lambda b, h: (b, h, 0)),
            pl.BlockSpec((1, seq, 2 * HEAD_DIM), lambda b, h: (b, 0, h)),
            pl.BlockSpec((1, V_HEAD_DIM, seq), lambda b, h: (b, h, 0)),
        ],
        out_specs=pl.BlockSpec((1, seq, V_HEAD_DIM), lambda b, h: (b, 0, h)),
        out_shape=jax.ShapeDtypeStruct((batch, seq, V_DIM), bf),
        scratch_shapes=[
            pltpu.VMEM((seq, K_AUG), bf),
            pltpu.VMEM((V_AUG, seq), bf),
            pltpu.VMEM((nq, K_AUG, 2 * t), bf),
            pltpu.VMEM((nq + 1, V_AUG, 2 * t), f32),
            pltpu.VMEM((nq, 1, 2 * t), f32),
            pltpu.VMEM((t, 2 * t), f32),
            pltpu.VMEM((t, 2 * t), f32),
            pltpu.VMEM((t, 2 * t), bf),
            pltpu.VMEM((t, 2 * t), bf),
            pltpu.VMEM((1, 2 * t), f32),
            pltpu.VMEM((1, 2 * t), f32),
            pltpu.VMEM((t, 2 * t), f32),
        ],
        compiler_params=pltpu.CompilerParams(
            dimension_semantics=("arbitrary", "arbitrary"), vmem_limit_bytes=VMEM_LIMIT_BYTES),
        name="attn",
    )(slopes, fast, bound, lq, gq_col, gsub_row, qt, kn, vt)


def _mix_kernel(x_ref, u_ref, halo_ref, ya_ref, gate_ref, wpg_ref, ps_ref, wbp_ref, wba_ref,
                wout_ref, gffn_ref, wup_ref, wdown_ref, o_ref, *, tiles_per_seq):
    tm = TOKEN_TILE
    i = pl.program_id(0)
    tile_in_seq = i % tiles_per_seq

    halo = jnp.where(tile_in_seq == 0, 0.0, halo_ref[...].astype(jnp.float32))
    u = u_ref[...].astype(jnp.float32)
    ext = jnp.concatenate([halo, u], axis=0)
    pos = tile_in_seq * tm + lax.broadcasted_iota(jnp.int32, (tm, POOL_GROUP_DIM), 0)
    win = ext
    ys = []
    for g, w in enumerate(POOL_WINDOWS):
        win = win[:, POOL_GROUP_DIM * (1 if g else 0):]
        win = win + pltpu.roll(win, w // 2, 0)
        cnt = jnp.minimum(pos + 1, w).astype(jnp.float32)
        ug = u[:, g * POOL_GROUP_DIM:(g + 1) * POOL_GROUP_DIM]
        mixed = win[POOL_HALO:, :POOL_GROUP_DIM] / cnt - ug
        ys.append(jnp.dot(mixed.astype(jnp.bfloat16), wpg_ref[g],
                          preferred_element_type=jnp.float32))
    y_pool = (jnp.concatenate(ys, axis=1) * ps_ref[...]).astype(jnp.bfloat16)

    bp = jnp.dot(y_pool, wbp_ref[...], preferred_element_type=jnp.float32)
    ba = jnp.dot(ya_ref[...], wba_ref[...], preferred_element_type=jnp.float32)
    gates = gate_ref[...].astype(jnp.float32)
    merged = gates[:, :D_MODEL] * bp + gates[:, D_MODEL:] * ba
    x1 = x_ref[...] + jnp.dot(merged.astype(jnp.bfloat16), wout_ref[...],
                              preferred_element_type=jnp.float32)

    ms = jnp.mean(x1 * x1, axis=-1, keepdims=True)
    h2 = (x1 * lax.rsqrt(ms + EPS) * gffn_ref[...]).astype(jnp.bfloat16)
    y = x1
    for cidx in range(D_FF // FF_CHUNK):
        sl = slice(cidx * FF_CHUNK, (cidx + 1) * FF_CHUNK)
        up = jnp.maximum(jnp.dot(h2, wup_ref[:, sl], preferred_element_type=jnp.float32), 0.0)
        y = y + jnp.dot((up * up).astype(jnp.bfloat16), wdown_ref[sl, :],
                        preferred_element_type=jnp.float32)
    o_ref[...] = y


def _mix_call(x2d, u, y_attn, gates, w_pg, pool_scale, w_bp, w_ba, w_out, g_ffn, w_up, w_down,
              seq):
    tokens = x2d.shape[0]
    tm = TOKEN_TILE
    row = lambda i: (i, 0)
    halo_blocks = tm // POOL_HALO
    kernel = functools.partial(_mix_kernel, tiles_per_seq=seq // tm)
    return pl.pallas_call(
        kernel,
        grid=(tokens // tm,),
        in_specs=[
            pl.BlockSpec((tm, D_MODEL), row),
            pl.BlockSpec((tm, POOL_DIM), row),
            pl.BlockSpec((POOL_HALO, POOL_DIM),
                         lambda i: (jnp.maximum(i * halo_blocks - 1, 0), 0)),
            pl.BlockSpec((tm, V_DIM), row),
            pl.BlockSpec((tm, 2 * D_MODEL), row),
            _const_spec(w_pg.shape),
            _const_spec((1, POOL_DIM)),
            _const_spec(w_bp.shape),
            _const_spec(w_ba.shape),
            _const_spec(w_out.shape),
            _const_spec((1, D_MODEL)),
            _const_spec(w_up.shape),
            _const_spec(w_down.shape),
        ],
        out_specs=pl.BlockSpec((tm, D_MODEL), row),
        out_shape=jax.ShapeDtypeStruct((tokens, D_MODEL), jnp.float32),
        compiler_params=pltpu.CompilerParams(
            dimension_semantics=("arbitrary",), vmem_limit_bytes=VMEM_LIMIT_BYTES),
        name="mix",
    )(x2d, u, u, y_attn, gates, w_pg, pool_scale, w_bp, w_ba, w_out, g_ffn, w_up, w_down)


def kernel(x, g_mix, w_in, w_pool_grp, pool_scale, g_q, g_k, lambda_qk, g_sub, w_branch_pool,
           w_branch_attn, w_gate, b_gate, w_out, g_ffn, w_up, w_down):
    batch, seq, d_model = x.shape
    depth = w_in.shape[0]
    assert d_model == D_MODEL and seq % TOKEN_TILE == 0 and seq % ATTN_TILE == 0
    assert TOKEN_TILE % POOL_HALO == 0 and POOL_HALO >= max(POOL_WINDOWS) - 1
    bf = jnp.bfloat16
    f32 = jnp.float32

    slopes = jnp.asarray([2.0 ** (-8.0 * (i + 1) / N_HEADS) for i in range(N_HEADS)], f32)
    grp = jnp.arange(MXU_COLS_V7X) // HEAD_DIM
    e_blk = (grp[:, None] == grp[None, :]).astype(bf)

    x2d = x.reshape(batch * seq, d_model)
    for l in range(depth):
        lambda_init = 0.8 - 0.6 * math.exp(-0.3 * l)
        k_lo, v_lo = POOL_DIM + QK_DIM, POOL_DIM + 2 * QK_DIM
        w_pk = jnp.concatenate([w_in[l][:, :POOL_DIM], w_in[l][:, k_lo:v_lo]], axis=1).astype(bf)
        w_qvt = jnp.concatenate([w_in[l][:, POOL_DIM:k_lo], w_in[l][:, v_lo:]], axis=1).T.astype(bf)
        gk_row = jnp.tile(g_k[l].astype(f32), 2 * N_HEADS).reshape(1, QK_DIM)
        gq_col = (g_q[l].astype(f32) * (HEAD_DIM ** -0.5 * LOG2E)).reshape(HEAD_DIM, 1)
        gsub_row = (g_sub[l].astype(f32) * (1.0 - lambda_init)).reshape(1, V_HEAD_DIM)

        u, kn, qt, vt, gates = _proj_call(
            x2d, g_mix[l].reshape(1, d_model).astype(f32), w_pk, w_qvt, w_gate[l].astype(bf),
            b_gate[l].reshape(1, 2 * d_model).astype(f32), gk_row, e_blk, batch, seq)
        y_attn = _attn_call(slopes, lambda_qk[l].astype(f32), gq_col,
                            jnp.max(jnp.abs(g_k[l].astype(f32))), gsub_row, qt,
                            kn.reshape(batch, seq, QK_DIM), vt, lambda_init)
        x2d = _mix_call(
            x2d, u, y_attn.reshape(batch * seq, V_DIM), gates, w_pool_grp[l].astype(bf),
            pool_scale[l].reshape(1, POOL_DIM).astype(f32), w_branch_pool[l].astype(bf),
            w_branch_attn[l].astype(bf), w_out[l].astype(bf),
            g_ffn[l].reshape(1, d_model).astype(f32), w_up[l].astype(bf), w_down[l].astype(bf), seq)
    return x2d.reshape(batch, seq, d_model)
```

```python
import functools
import math

import jax
import jax.numpy as jnp
from jax import lax
from jax.experimental import pallas as pl
from jax.experimental.pallas import tpu as pltpu

D_MODEL = 1024
CHUNK = 64
POOL_WINDOWS = (2, 4, 8, 16)
N_POOL_GROUPS = len(POOL_WINDOWS)
POOL_GROUP_DIM = 128
POOL_DIM = N_POOL_GROUPS * POOL_GROUP_DIM
N_HEADS = 8
HEAD_DIM = 64
V_HEAD_DIM = 2 * HEAD_DIM
QK_DIM = N_HEADS * 2 * HEAD_DIM
V_DIM = N_HEADS * V_HEAD_DIM
D_FF = 4 * D_MODEL
EPS = 1e-6
LOG2E = math.log2(math.e)

MXU_COLS_V7X = 256
BF16_ROWS_V7X = 16
VMEM_LIMIT_BYTES = 56 * 1024 * 1024

TOKEN_TILE = 512
ATTN_TILE = 256
FAST_UNROLL = 28
FAST_DIAG_UNROLL = 8
TILES_PER_TRIP = 4
DIAG_UNROLL = 4
PAIR_UNROLL = 8
FAST_BOUND_MAX = 48.0
BOUND_MARGIN = 1.02
POOL_HALO = 16
FF_CHUNK = 1024
MASK_VALUE = -1e30

K_AUG = 2 * HEAD_DIM + MXU_COLS_V7X // 2
N_PIECES = 3
V_AUG = V_HEAD_DIM + BF16_ROWS_V7X


def _const_spec(shape):
    nd = len(shape)
    return pl.BlockSpec(shape, lambda *_: (0,) * nd, pipeline_mode=pl.Buffered(1))


def _proj_kernel(x_ref, g_ref, wpk_ref, wqvt_ref, wg_ref, bg_ref, gk_ref, e_ref,
                 u_ref, kn_ref, qt_ref, vt_ref, gate_ref):
    xf = x_ref[...]
    ms = jnp.mean(xf * xf, axis=-1, keepdims=True)
    h = (xf * lax.rsqrt(ms + EPS) * g_ref[...]).astype(jnp.bfloat16)

    zpk = jnp.dot(h, wpk_ref[...], preferred_element_type=jnp.float32)
    u_ref[...] = zpk[:, :POOL_DIM].astype(u_ref.dtype)

    zk = zpk[:, POOL_DIM:]
    sq = zk * zk
    sq_hi = sq.astype(jnp.bfloat16)
    sq_lo = (sq - sq_hi.astype(jnp.float32)).astype(jnp.bfloat16)
    e = e_ref[...]
    parts = []
    for c in range(QK_DIM // MXU_COLS_V7X):
        sl = slice(c * MXU_COLS_V7X, (c + 1) * MXU_COLS_V7X)
        parts.append(jnp.dot(sq_hi[:, sl], e, preferred_element_type=jnp.float32)
                     + jnp.dot(sq_lo[:, sl], e, preferred_element_type=jnp.float32))
    ssq = jnp.concatenate(parts, axis=1)
    kn = zk * lax.rsqrt(ssq * (1.0 / HEAD_DIM) + EPS) * gk_ref[...]
    kn_ref[...] = kn.astype(kn_ref.dtype)

    qvt = lax.dot_general(wqvt_ref[...], h, (((1,), (1,)), ((), ())),
                          preferred_element_type=jnp.float32)
    qt_ref[0] = qvt[:QK_DIM].astype(qt_ref.dtype)
    vt_ref[0] = qvt[QK_DIM:].astype(vt_ref.dtype)

    zg = jnp.dot(h, wg_ref[...], preferred_element_type=jnp.float32) + bg_ref[...]
    gate_ref[...] = (1.0 / (1.0 + jnp.exp(-zg))).astype(gate_ref.dtype)


def _proj_call(x2d, g_mix, w_pk, w_qvt, w_gate, b_gate, gk_row, e_blk, batch, seq):
    tokens = x2d.shape[0]
    tm = TOKEN_TILE
    nst = seq // tm
    bf = jnp.bfloat16
    row = lambda i: (i, 0)
    tr = lambda i: (i // nst, 0, i % nst)
    return pl.pallas_call(
        _proj_kernel,
        grid=(tokens // tm,),
        in_specs=[
            pl.BlockSpec((tm, D_MODEL), row),
            _const_spec((1, D_MODEL)),
            _const_spec(w_pk.shape),
            _const_spec(w_qvt.shape),
            _const_spec(w_gate.shape),
            _const_spec((1, 2 * D_MODEL)),
            _const_spec((1, QK_DIM)),
            _const_spec(e_blk.shape),
        ],
        out_specs=[
            pl.BlockSpec((tm, POOL_DIM), row),
            pl.BlockSpec((tm, QK_DIM), row),
            pl.BlockSpec((1, QK_DIM, tm), tr),
            pl.BlockSpec((1, V_DIM, tm), tr),
            pl.BlockSpec((tm, 2 * D_MODEL), row),
        ],
        out_shape=[
            jax.ShapeDtypeStruct((tokens, POOL_DIM), bf),
            jax.ShapeDtypeStruct((tokens, QK_DIM), bf),
            jax.ShapeDtypeStruct((batch, QK_DIM, seq), bf),
            jax.ShapeDtypeStruct((batch, V_DIM, seq), bf),
            jax.ShapeDtypeStruct((tokens, 2 * D_MODEL), bf),
        ],
        compiler_params=pltpu.CompilerParams(
            dimension_semantics=("arbitrary",), vmem_limit_bytes=VMEM_LIMIT_BYTES),
        name="proj",
    )(x2d, g_mix, w_pk, w_qvt, w_gate, b_gate, gk_row, e_blk)


def _attn_kernel(slopes_ref, fast_ref, bound_ref, lq_ref, gq_ref, gsub_ref, qt_ref, k_ref, vt_ref,
                 o_ref, kaug, vaug, wq, acc, m_all, s_even, s_odd, p_even, p_odd, a_even, a_odd,
                 dbias, s2_even, s2_odd, p2_even, p2_odd, *, seq, lambda_init):
    t = ATTN_TILE
    nq = seq // t
    slope2 = slopes_ref[pl.program_id(1)] * LOG2E
    n_extra = K_AUG - 2 * HEAD_DIM

    kaug[:, :2 * HEAD_DIM] = k_ref[0]
    row = lax.broadcasted_iota(jnp.int32, (t, n_extra), 0).astype(jnp.float32)
    lane = lax.broadcasted_iota(jnp.int32, (t, n_extra), 1)
    for i in range(nq):
        cols = jnp.where(lane < N_PIECES, row,
                         jnp.where(lane < 2 * N_PIECES, float(i),
                                   jnp.where(lane < 3 * N_PIECES, 1.0, 0.0)))
        kaug[i * t:(i + 1) * t, 2 * HEAD_DIM:] = cols.astype(kaug.dtype)
    vaug[:V_HEAD_DIM, :] = vt_ref[0]
    vaug[V_HEAD_DIM:, :] = jnp.ones((V_AUG - V_HEAD_DIM, seq), vaug.dtype)

    def pieces(x):
        hi = x.astype(jnp.bfloat16).astype(jnp.float32)
        mid = (x - hi).astype(jnp.bfloat16).astype(jnp.float32)
        return hi, mid, x - hi - mid

    def stack_rows(groups):
        prow = lax.broadcasted_iota(jnp.int32, (BF16_ROWS_V7X, 2 * t), 0)
        out = jnp.zeros((BF16_ROWS_V7X, 2 * t), jnp.float32)
        for idx, rowv in enumerate(groups):
            out = jnp.where(prow == idx, rowv, out)
        return out.astype(wq.dtype)

    slope_pieces = pieces(jnp.full((1, 2 * t), slope2, jnp.float32))

    r = lax.broadcasted_iota(jnp.int32, (t, t), 0)
    c = lax.broadcasted_iota(jnp.int32, (t, t), 1)
    ahead = slope2 * (2.0 * (c - r).astype(jnp.float32))
    one_map = jnp.where((r // CHUNK) <= (c // CHUNK), jnp.where(r <= c, 0.0, ahead), MASK_VALUE)
    dbias[:, :t] = one_map
    dbias[:, t:] = one_map

    lq = lq_ref[...]
    lam = (jnp.exp(jnp.sum(lq[0:1] * lq[1:2], axis=1, keepdims=True))
           - jnp.exp(jnp.sum(lq[2:3] * lq[3:4], axis=1, keepdims=True)) + lambda_init)
    gq = gq_ref[...]

    def build_wq(qtile, position_rows):
        q0 = pl.multiple_of(qtile * t, t)
        qt = qt_ref[0, :, pl.ds(q0, t)].astype(jnp.float32)
        zeros = jnp.zeros((HEAD_DIM, t), jnp.float32)
        for half in range(2):
            qh = qt[half * HEAD_DIM:(half + 1) * HEAD_DIM]
            qh = qh * lax.rsqrt(jnp.mean(qh * qh, axis=0, keepdims=True) + EPS) * gq
            both = jnp.concatenate([zeros, qh] if half else [qh, zeros], axis=1)
            wq[qtile, half * HEAD_DIM:(half + 1) * HEAD_DIM, :] = both.astype(wq.dtype)
        wq[qtile, 2 * HEAD_DIM:2 * HEAD_DIM + BF16_ROWS_V7X, :] = position_rows(qtile)
        wq[qtile, 2 * HEAD_DIM + BF16_ROWS_V7X:, :] = jnp.zeros(
            (n_extra - BF16_ROWS_V7X, 2 * t), wq.dtype)

    def per_tile_pair(fn):
        def body(i, carry):
            for u in range(TILES_PER_TRIP):
                fn(TILES_PER_TRIP * i + u)
            return carry
        lax.fori_loop(0, nq // TILES_PER_TRIP, body, 0)

    def finalize(qtile):
        q0 = pl.multiple_of(qtile * t, t)
        a = acc[qtile]
        o2 = a[:V_HEAD_DIM] * (1.0 / a[V_HEAD_DIM:V_HEAD_DIM + 1])
        o = o2[:, :t] - lam * o2[:, t:]
        o = o * lax.rsqrt(jnp.mean(o * o, axis=0, keepdims=True) + EPS)
        o_ref[0, pl.ds(q0, t), :] = (o.T * gsub_ref[...]).astype(o_ref.dtype)

    i32 = jnp.int32

    @pl.when(fast_ref[0] == 1)
    def _():
        _attn_fast_sweep(t, nq, slope2, bound_ref[0], slope_pieces, pieces, stack_rows, build_wq,
                         per_tile_pair, kaug, vaug, wq, acc, dbias,
                         ((s_even, p_even), (s_odd, p_odd)), ((s2_even, p2_even), (s2_odd, p2_odd)))

    @pl.when(fast_ref[0] == 0)
    def _():
        _attn_safe_sweep(t, nq, slope2, slope_pieces, stack_rows, build_wq, per_tile_pair, kaug,
                         vaug, wq, acc, m_all, s_even, s_odd, p_even, p_odd, a_even, a_odd, dbias)

    per_tile_pair(finalize)


def _attn_fast_sweep(t, nq, slope2, bound, slope_pieces, pieces, stack_rows, build_wq,
                     per_tile_pair, kaug, vaug, wq, acc, dbias, one_tile, two_tiles):
    tile_pieces = tuple(p * float(t) for p in slope_pieces)
    col = lax.broadcasted_iota(jnp.int32, (1, 2 * t), 1)
    col = jnp.where(col >= t, col - t, col)

    def position_rows(qtile):
        qpos = (qtile * t + col).astype(jnp.float32)
        return stack_rows(slope_pieces + tile_pieces + pieces(-(slope2 * qpos) - bound))

    per_tile_pair(functools.partial(build_wq, position_rows=position_rows))

    def sweep(first, advance, n_steps, unroll, diag, ktiles, bufs):
        assert n_steps % unroll == 0 and unroll % 2 == 0
        rows = ktiles * t

        def scores(pair, s_dst):
            k0 = pl.multiple_of(pair[1] * t, t)
            s_dst[...] = jnp.dot(kaug[pl.ds(k0, rows), :], wq[pair[0]],
                                 preferred_element_type=jnp.float32)

        def probabilities(s_src, p_dst):
            below = rows - t if diag else rows
            if below:
                p_dst[:below] = jnp.exp2(s_src[:below]).astype(p_dst.dtype)
            if diag:
                p_dst[below:] = jnp.exp2(s_src[below:] + dbias[...]).astype(p_dst.dtype)

        def weighted_values(pair, p_src):
            qtile, ktile = pair
            k0 = pl.multiple_of(ktile * t, t)
            pv = jnp.dot(vaug[:, pl.ds(k0, rows)], p_src[...], preferred_element_type=jnp.float32)
            acc[qtile] = pv if diag else acc[qtile] + pv

        def clamped(pair):
            q, j = advance(pair)
            return jnp.minimum(q, nq - 1), jnp.minimum(j, nq - ktiles)

        (s_0, p_0), (s_1, p_1) = bufs
        second = clamped(first)
        scores(first, s_0)
        probabilities(s_0, p_0)
        scores(second, s_1)

        def body(_, state):
            for k in range(unroll):
                (s_a, p_a), (s_b, p_b) = bufs[k % 2], bufs[(k + 1) % 2]
                cur, nxt = state
                nxt2 = clamped(nxt)
                scores(nxt2, s_a)
                probabilities(s_b, p_b)
                weighted_values(cur, p_a)
                state = (nxt, nxt2)
            return state

        lax.fori_loop(0, n_steps // unroll, body, (first, second))

    i32 = jnp.int32
    step_two = lambda pair: (pair[0] + 2, pair[1] + 2)
    sweep((i32(0), i32(0)), step_two, nq // 2, FAST_DIAG_UNROLL, True, 1, one_tile)
    sweep((i32(1), i32(0)), step_two, nq // 2, FAST_DIAG_UNROLL, True, 2, two_tiles)

    def next_two(pair):
        q, j = pair
        wrap = j + 2 >= q - (q & 1)
        return jnp.where(wrap, q + 1, q), jnp.where(wrap, 0, j + 2)

    n_double = sum(q // 2 for q in range(nq))
    sweep((i32(2), i32(0)), next_two, n_double, FAST_UNROLL, False, 2, two_tiles)


def _attn_safe_sweep(t, nq, slope2, slope_pieces, stack_rows, build_wq, per_tile_pair, kaug, vaug,
                     wq, acc, m_all, s_even, s_odd, p_even, p_odd, a_even, a_odd, dbias):
    only_offset_rows = stack_rows(slope_pieces)
    per_tile_pair(functools.partial(build_wq, position_rows=lambda qtile: only_offset_rows))

    def scores(qtile, ktile, s_dst):
        k0 = pl.multiple_of(ktile * t, t)
        s_dst[...] = jnp.dot(kaug[pl.ds(k0, t), :], wq[qtile],
                             preferred_element_type=jnp.float32)

    def softmax(qtile, ktile, s_src, p_dst, a_dst, diag):
        s = s_src[...]
        if diag:
            s = s + dbias[...]
            m_new = jnp.max(s, axis=0, keepdims=True)
            p_dst[...] = jnp.exp2(s - m_new).astype(p_dst.dtype)
        else:
            shift = slope2 * ((ktile - qtile) * t).astype(jnp.float32)
            m_old = m_all[qtile]
            m_new = jnp.maximum(m_old, jnp.max(s, axis=0, keepdims=True) + shift)
            p_dst[...] = jnp.exp2(s - (m_new - shift)).astype(p_dst.dtype)
            a_dst[...] = jnp.exp2(m_old - m_new)
        m_all[qtile] = m_new

    def weighted_values(qtile, ktile, p_src, a_src, diag):
        k0 = pl.multiple_of(ktile * t, t)
        pv = jnp.dot(vaug[:, pl.ds(k0, t)], p_src[...], preferred_element_type=jnp.float32)
        acc[qtile] = pv if diag else a_src[...] * acc[qtile] + pv

    bufs = ((s_even, p_even, a_even), (s_odd, p_odd, a_odd))

    spare = (jnp.int32(nq), jnp.int32(0))
    p_odd[...] = jnp.zeros_like(p_odd)
    a_odd[...] = jnp.ones_like(a_odd)
    acc[nq] = jnp.zeros(acc.shape[1:], acc.dtype)

    def run_pairs(first, advance, n_steps, unroll, diag):
        assert n_steps % unroll == 0 and unroll % 2 == 0
        scores(*first, s_even)

        def body(_, state):
            for k in range(unroll):
                (s_cur, p_cur, a_cur), (s_nxt, p_prev, a_prev) = bufs[k % 2], bufs[(k + 1) % 2]
                q, j, qp, jp = state
                qn, jn = advance(q, j)
                qn, jn = jnp.minimum(qn, nq - 1), jnp.minimum(jn, nq - 1)
                scores(qn, jn, s_nxt)
                softmax(q, j, s_cur, p_cur, a_cur, diag)
                weighted_values(qp, jp, p_prev, a_prev, diag)
                state = (qn, jn, q, j)
            return state

        state = lax.fori_loop(0, n_steps // unroll, body, (*first, *spare))
        weighted_values(state[2], state[3], p_odd, a_odd, diag)

    i32 = jnp.int32
    run_pairs((i32(0), i32(0)), lambda q, j: (q + 1, j + 1), nq, DIAG_UNROLL, True)

    def below_diag(q, j):
        wrap = j + 1 == q
        return jnp.where(wrap, q + 1, q), jnp.where(wrap, 0, j + 1)

    run_pairs((i32(1), i32(0)), below_diag, nq * (nq - 1) // 2, PAIR_UNROLL, False)


def _attn_call(slopes, lq, gq_col, gk_max, gsub_row, qt, kn, vt, lambda_init):
    batch, _, seq = qt.shape
    t = ATTN_TILE
    nq = seq // t
    assert nq % TILES_PER_TRIP == 0
    assert nq % DIAG_UNROLL == 0 and (nq * (nq - 1) // 2) % PAIR_UNROLL == 0
    assert (nq // 2) % FAST_DIAG_UNROLL == 0 and sum(q // 2 for q in range(nq)) % FAST_UNROLL == 0
    assert t <= 256, "in-tile key offsets must be exact in bf16"
    kernel = functools.partial(_attn_kernel, seq=seq, lambda_init=lambda_init)
    f32, bf = jnp.float32, jnp.bfloat16
    bound = (BOUND_MARGIN * HEAD_DIM * jnp.max(jnp.abs(gq_col)) * gk_max).astype(f32).reshape(1)
    fast = (bound <= FAST_BOUND_MAX).astype(jnp.int32)
    return pl.pallas_call(
        kernel,
        grid=(batch, N_HEADS),
        in_specs=[
            pl.BlockSpec(memory_space=pltpu.SMEM),
            pl.BlockSpec(memory_space=pltpu.SMEM),
            pl.BlockSpec(memory_space=pltpu.SMEM),
            _const_spec(lq.shape),
            _const_spec(gq_col.shape),
            _const_spec(gsub_row.shape),
            pl.BlockSpec((1, 2 * HEAD_DIM, seq), lambda b, h: (b, h, 0)),
            pl.BlockSpec((1, seq, 2 * HEAD_DIM), lambda b, h: (b, 0, h)),
            pl.BlockSpec((1, V_HEAD_DIM, seq), lambda b, h: (b, h, 0)),
        ],
        out_specs=pl.BlockSpec((1, seq, V_HEAD_DIM), lambda b, h: (b, 0, h)),
        out_shape=jax.ShapeDtypeStruct((batch, seq, V_DIM), bf),
        scratch_shapes=[
            pltpu.VMEM((seq, K_AUG), bf),
            pltpu.VMEM((V_AUG, seq), bf),
            pltpu.VMEM((nq, K_AUG, 2 * t), bf),
            pltpu.VMEM((nq + 1, V_AUG, 2 * t), f32),
            pltpu.VMEM((nq, 1, 2 * t), f32),
            pltpu.VMEM((t, 2 * t), f32),
            pltpu.VMEM((t, 2 * t), f32),
            pltpu.VMEM((t, 2 * t), bf),
            pltpu.VMEM((t, 2 * t), bf),
            pltpu.VMEM((1, 2 * t), f32),
            pltpu.VMEM((1, 2 * t), f32),
            pltpu.VMEM((t, 2 * t), f32),
            pltpu.VMEM((2 * t, 2 * t), f32),
            pltpu.VMEM((2 * t, 2 * t), f32),
            pltpu.VMEM((2 * t, 2 * t), bf),
            pltpu.VMEM((2 * t, 2 * t), bf),
        ],
        compiler_params=pltpu.CompilerParams(
            dimension_semantics=("arbitrary", "arbitrary"), vmem_limit_bytes=VMEM_LIMIT_BYTES),
        name="attn",
    )(slopes, fast, bound, lq, gq_col, gsub_row, qt, kn, vt)


def _mix_kernel(x_ref, u_ref, halo_ref, ya_ref, gate_ref, wpg_ref, ps_ref, wbp_ref, wba_ref,
                wout_ref, gffn_ref, wup_ref, wdown_ref, o_ref, *, tiles_per_seq):
    tm = TOKEN_TILE
    i = pl.program_id(0)
    tile_in_seq = i % tiles_per_seq

    halo = jnp.where(tile_in_seq == 0, 0.0, halo_ref[...].astype(jnp.float32))
    u = u_ref[...].astype(jnp.float32)
    ext = jnp.concatenate([halo, u], axis=0)
    pos = tile_in_seq * tm + lax.broadcasted_iota(jnp.int32, (tm, POOL_GROUP_DIM), 0)
    win = ext
    ys = []
    for g, w in enumerate(POOL_WINDOWS):
        win = win[:, POOL_GROUP_DIM * (1 if g else 0):]
        win = win + pltpu.roll(win, w // 2, 0)
        cnt = jnp.minimum(pos + 1, w).astype(jnp.float32)
        ug = u[:, g * POOL_GROUP_DIM:(g + 1) * POOL_GROUP_DIM]
        mixed = win[POOL_HALO:, :POOL_GROUP_DIM] / cnt - ug
        ys.append(jnp.dot(mixed.astype(jnp.bfloat16), wpg_ref[g],
                          preferred_element_type=jnp.float32))
    y_pool = (jnp.concatenate(ys, axis=1) * ps_ref[...]).astype(jnp.bfloat16)

    bp = jnp.dot(y_pool, wbp_ref[...], preferred_element_type=jnp.float32)
    ba = jnp.dot(ya_ref[...], wba_ref[...], preferred_element_type=jnp.float32)
    gates = gate_ref[...].astype(jnp.float32)
    merged = gates[:, :D_MODEL] * bp + gates[:, D_MODEL:] * ba
    x1 = x_ref[...] + jnp.dot(merged.astype(jnp.bfloat16), wout_ref[...],
                              preferred_element_type=jnp.float32)

    ms = jnp.mean(x1 * x1, axis=-1, keepdims=True)
    h2 = (x1 * lax.rsqrt(ms + EPS) * gffn_ref[...]).astype(jnp.bfloat16)
    y = x1
    for cidx in range(D_FF // FF_CHUNK):
        sl = slice(cidx * FF_CHUNK, (cidx + 1) * FF_CHUNK)
        up = jnp.maximum(jnp.dot(h2, wup_ref[:, sl], preferred_element_type=jnp.float32), 0.0)
        y = y + jnp.dot((up * up).astype(jnp.bfloat16), wdown_ref[sl, :],
                        preferred_element_type=jnp.float32)
    o_ref[...] = y


def _mix_call(x2d, u, y_attn, gates, w_pg, pool_scale, w_bp, w_ba, w_out, g_ffn, w_up, w_down,
              seq):
    tokens = x2d.shape[0]
    tm = TOKEN_TILE
    row = lambda i: (i, 0)
    halo_blocks = tm // POOL_HALO
    kernel = functools.partial(_mix_kernel, tiles_per_seq=seq // tm)
    return pl.pallas_call(
        kernel,
        grid=(tokens // tm,),
        in_specs=[
            pl.BlockSpec((tm, D_MODEL), row),
            pl.BlockSpec((tm, POOL_DIM), row),
            pl.BlockSpec((POOL_HALO, POOL_DIM),
                         lambda i: (jnp.maximum(i * halo_blocks - 1, 0), 0)),
            pl.BlockSpec((tm, V_DIM), row),
            pl.BlockSpec((tm, 2 * D_MODEL), row),
            _const_spec(w_pg.shape),
            _const_spec((1, POOL_DIM)),
            _const_spec(w_bp.shape),
            _const_spec(w_ba.shape),
            _const_spec(w_out.shape),
            _const_spec((1, D_MODEL)),
            _const_spec(w_up.shape),
            _const_spec(w_down.shape),
        ],
        out_specs=pl.BlockSpec((tm, D_MODEL), row),
        out_shape=jax.ShapeDtypeStruct((tokens, D_MODEL), jnp.float32),
        compiler_params=pltpu.CompilerParams(
            dimension_semantics=("arbitrary",), vmem_limit_bytes=VMEM_LIMIT_BYTES),
        name="mix",
    )(x2d, u, u, y_attn, gates, w_pg, pool_scale, w_bp, w_ba, w_out, g_ffn, w_up, w_down)


def kernel(x, g_mix, w_in, w_pool_grp, pool_scale, g_q, g_k, lambda_qk, g_sub, w_branch_pool,
           w_branch_attn, w_gate, b_gate, w_out, g_ffn, w_up, w_down):
    batch, seq, d_model = x.shape
    depth = w_in.shape[0]
    assert d_model == D_MODEL and seq % TOKEN_TILE == 0 and seq % ATTN_TILE == 0
    assert TOKEN_TILE % POOL_HALO == 0 and POOL_HALO >= max(POOL_WINDOWS) - 1
    bf = jnp.bfloat16
    f32 = jnp.float32

    slopes = jnp.asarray([2.0 ** (-8.0 * (i + 1) / N_HEADS) for i in range(N_HEADS)], f32)
    grp = jnp.arange(MXU_COLS_V7X) // HEAD_DIM
    e_blk = (grp[:, None] == grp[None, :]).astype(bf)

    x2d = x.reshape(batch * seq, d_model)
    for l in range(depth):
        lambda_init = 0.8 - 0.6 * math.exp(-0.3 * l)
        k_lo, v_lo = POOL_DIM + QK_DIM, POOL_DIM + 2 * QK_DIM
        w_pk = jnp.concatenate([w_in[l][:, :POOL_DIM], w_in[l][:, k_lo:v_lo]], axis=1).astype(bf)
        w_qvt = jnp.concatenate([w_in[l][:, POOL_DIM:k_lo], w_in[l][:, v_lo:]], axis=1).T.astype(bf)
        gk_row = jnp.tile(g_k[l].astype(f32), 2 * N_HEADS).reshape(1, QK_DIM)
        gq_col = (g_q[l].astype(f32) * (HEAD_DIM ** -0.5 * LOG2E)).reshape(HEAD_DIM, 1)
        gsub_row = (g_sub[l].astype(f32) * (1.0 - lambda_init)).reshape(1, V_HEAD_DIM)

        u, kn, qt, vt, gates = _proj_call(
            x2d, g_mix[l].reshape(1, d_model).astype(f32), w_pk, w_qvt, w_gate[l].astype(bf),
            b_gate[l].reshape(1, 2 * d_model).astype(f32), gk_row, e_blk, batch, seq)
        y_attn = _attn_call(slopes, lambda_qk[l].astype(f32), gq_col,
                            jnp.max(jnp.abs(g_k[l].astype(f32))), gsub_row, qt,
                            kn.reshape(batch, seq, QK_DIM), vt, lambda_init)
        x2d = _mix_call(
            x2d, u, y_attn.reshape(batch * seq, V_DIM), gates, w_pool_grp[l].astype(bf),
            pool_scale[l].reshape(1, POOL_DIM).astype(f32), w_branch_pool[l].astype(bf),
            w_branch_attn[l].astype(bf), w_out[l].astype(bf),
            g_ffn[l].reshape(1, d_model).astype(f32), w_up[l].astype(bf), w_down[l].astype(bf), seq)
    return x2d.reshape(batch, seq, d_model)
```

```python
import functools
import math

import jax
import jax.numpy as jnp
from jax import lax
from jax.experimental import pallas as pl
from jax.experimental.pallas import tpu as pltpu

D_MODEL = 1024
CHUNK = 64
POOL_WINDOWS = (2, 4, 8, 16)
N_POOL_GROUPS = len(POOL_WINDOWS)
POOL_GROUP_DIM = 128
POOL_DIM = N_POOL_GROUPS * POOL_GROUP_DIM
N_HEADS = 8
HEAD_DIM = 64
V_HEAD_DIM = 2 * HEAD_DIM
QK_DIM = N_HEADS * 2 * HEAD_DIM
V_DIM = N_HEADS * V_HEAD_DIM
D_FF = 4 * D_MODEL
EPS = 1e-6
LOG2E = math.log2(math.e)

MXU_COLS_V7X = 256
BF16_ROWS_V7X = 16
VMEM_LIMIT_BYTES = 56 * 1024 * 1024

TOKEN_TILE = 512
ATTN_TILE = 256
TILES_PER_TRIP = 4
DIAG_UNROLL = 4
PAIR_UNROLL = 8
FAST_BOUND_MAX = 48.0
BOUND_MARGIN = 1.02
POOL_HALO = 16
FF_CHUNK = 1024
MASK_VALUE = -1e30

K_AUG = 2 * HEAD_DIM + MXU_COLS_V7X // 2
N_PIECES = 3
V_AUG = V_HEAD_DIM + BF16_ROWS_V7X


def _const_spec(shape):
    nd = len(shape)
    return pl.BlockSpec(shape, lambda *_: (0,) * nd, pipeline_mode=pl.Buffered(1))


def _proj_kernel(x_ref, g_ref, wpk_ref, wqvt_ref, wg_ref, bg_ref, gk_ref, e_ref,
                 u_ref, kn_ref, qt_ref, vt_ref, gate_ref):
    xf = x_ref[...]
    ms = jnp.mean(xf * xf, axis=-1, keepdims=True)
    h = (xf * lax.rsqrt(ms + EPS) * g_ref[...]).astype(jnp.bfloat16)

    zpk = jnp.dot(h, wpk_ref[...], preferred_element_type=jnp.float32)
    u_ref[...] = zpk[:, :POOL_DIM].astype(u_ref.dtype)

    zk = zpk[:, POOL_DIM:]
    sq = zk * zk
    sq_hi = sq.astype(jnp.bfloat16)
    sq_lo = (sq - sq_hi.astype(jnp.float32)).astype(jnp.bfloat16)
    e = e_ref[...]
    parts = []
    for c in range(QK_DIM // MXU_COLS_V7X):
        sl = slice(c * MXU_COLS_V7X, (c + 1) * MXU_COLS_V7X)
        parts.append(jnp.dot(sq_hi[:, sl], e, preferred_element_type=jnp.float32)
                     + jnp.dot(sq_lo[:, sl], e, preferred_element_type=jnp.float32))
    ssq = jnp.concatenate(parts, axis=1)
    kn = zk * lax.rsqrt(ssq * (1.0 / HEAD_DIM) + EPS) * gk_ref[...]
    kn_ref[...] = kn.astype(kn_ref.dtype)

    qvt = lax.dot_general(wqvt_ref[...], h, (((1,), (1,)), ((), ())),
                          preferred_element_type=jnp.float32)
    qt_ref[0] = qvt[:QK_DIM].astype(qt_ref.dtype)
    vt_ref[0] = qvt[QK_DIM:].astype(vt_ref.dtype)

    zg = jnp.dot(h, wg_ref[...], preferred_element_type=jnp.float32) + bg_ref[...]
    gate_ref[...] = (1.0 / (1.0 + jnp.exp(-zg))).astype(gate_ref.dtype)


def _proj_call(x2d, g_mix, w_pk, w_qvt, w_gate, b_gate, gk_row, e_blk, batch, seq):
    tokens = x2d.shape[0]
    tm = TOKEN_TILE
    nst = seq // tm
    bf = jnp.bfloat16
    row = lambda i: (i, 0)
    tr = lambda i: (i // nst, 0, i % nst)
    return pl.pallas_call(
        _proj_kernel,
        grid=(tokens // tm,),
        in_specs=[
            pl.BlockSpec((tm, D_MODEL), row),
            _const_spec((1, D_MODEL)),
            _const_spec(w_pk.shape),
            _const_spec(w_qvt.shape),
            _const_spec(w_gate.shape),
            _const_spec((1, 2 * D_MODEL)),
            _const_spec((1, QK_DIM)),
            _const_spec(e_blk.shape),
        ],
        out_specs=[
            pl.BlockSpec((tm, POOL_DIM), row),
            pl.BlockSpec((tm, QK_DIM), row),
            pl.BlockSpec((1, QK_DIM, tm), tr),
            pl.BlockSpec((1, V_DIM, tm), tr),
            pl.BlockSpec((tm, 2 * D_MODEL), row),
        ],
        out_shape=[
            jax.ShapeDtypeStruct((tokens, POOL_DIM), bf),
            jax.ShapeDtypeStruct((tokens, QK_DIM), bf),
            jax.ShapeDtypeStruct((batch, QK_DIM, seq), bf),
            jax.ShapeDtypeStruct((batch, V_DIM, seq), bf),
            jax.ShapeDtypeStruct((tokens, 2 * D_MODEL), bf),
        ],
        compiler_params=pltpu.CompilerParams(
            dimension_semantics=("arbitrary",), vmem_limit_bytes=VMEM_LIMIT_BYTES),
        name="proj",
    )(x2d, g_mix, w_pk, w_qvt, w_gate, b_gate, gk_row, e_blk)


def _attn_kernel(slopes_ref, fast_ref, bound_ref, lq_ref, gq_ref, gsub_ref, qt_ref, k_ref, vt_ref,
                 o_ref, kaug, vaug, wq, acc, m_all, s_even, s_odd, p_even, p_odd, a_even, a_odd,
                 dbias, s2_even, s2_odd, p2_even, p2_odd, *, seq, lambda_init):
    t = ATTN_TILE
    nq = seq // t
    slope2 = slopes_ref[pl.program_id(1)] * LOG2E
    n_extra = K_AUG - 2 * HEAD_DIM

    kaug[:, :2 * HEAD_DIM] = k_ref[0]
    row = lax.broadcasted_iota(jnp.int32, (t, n_extra), 0).astype(jnp.float32)
    lane = lax.broadcasted_iota(jnp.int32, (t, n_extra), 1)
    for i in range(nq):
        cols = jnp.where(lane < N_PIECES, row,
                         jnp.where(lane < 2 * N_PIECES, float(i),
                                   jnp.where(lane < 3 * N_PIECES, 1.0, 0.0)))
        kaug[i * t:(i + 1) * t, 2 * HEAD_DIM:] = cols.astype(kaug.dtype)
    vaug[:V_HEAD_DIM, :] = vt_ref[0]
    vaug[V_HEAD_DIM:, :] = jnp.ones((V_AUG - V_HEAD_DIM, seq), vaug.dtype)

    def pieces(x):
        hi = x.astype(jnp.bfloat16).astype(jnp.float32)
        mid = (x - hi).astype(jnp.bfloat16).astype(jnp.float32)
        return hi, mid, x - hi - mid

    def stack_rows(groups):
        prow = lax.broadcasted_iota(jnp.int32, (BF16_ROWS_V7X, 2 * t), 0)
        out = jnp.zeros((BF16_ROWS_V7X, 2 * t), jnp.float32)
        for idx, rowv in enumerate(groups):
            out = jnp.where(prow == idx, rowv, out)
        return out.astype(wq.dtype)

    slope_pieces = pieces(jnp.full((1, 2 * t), slope2, jnp.float32))

    r = lax.broadcasted_iota(jnp.int32, (t, t), 0)
    c = lax.broadcasted_iota(jnp.int32, (t, t), 1)
    ahead = slope2 * (2.0 * (c - r).astype(jnp.float32))
    one_map = jnp.where((r // CHUNK) <= (c // CHUNK), jnp.where(r <= c, 0.0, ahead), MASK_VALUE)
    dbias[:, :t] = one_map
    dbias[:, t:] = one_map

    lq = lq_ref[...]
    lam = (jnp.exp(jnp.sum(lq[0:1] * lq[1:2], axis=1, keepdims=True))
           - jnp.exp(jnp.sum(lq[2:3] * lq[3:4], axis=1, keepdims=True)) + lambda_init)
    gq = gq_ref[...]

    def tile_start(tile):
        return tile * t if isinstance(tile, int) else pl.multiple_of(tile * t, t)

    def build_wq(qtile, position_rows):
        q0 = tile_start(qtile)
        qt = qt_ref[0, :, pl.ds(q0, t)].astype(jnp.float32)
        zeros = jnp.zeros((HEAD_DIM, t), jnp.float32)
        for half in range(2):
            qh = qt[half * HEAD_DIM:(half + 1) * HEAD_DIM]
            qh = qh * lax.rsqrt(jnp.mean(qh * qh, axis=0, keepdims=True) + EPS) * gq
            both = jnp.concatenate([zeros, qh] if half else [qh, zeros], axis=1)
            wq[qtile, half * HEAD_DIM:(half + 1) * HEAD_DIM, :] = both.astype(wq.dtype)
        wq[qtile, 2 * HEAD_DIM:2 * HEAD_DIM + BF16_ROWS_V7X, :] = position_rows(qtile)
        wq[qtile, 2 * HEAD_DIM + BF16_ROWS_V7X:, :] = jnp.zeros(
            (n_extra - BF16_ROWS_V7X, 2 * t), wq.dtype)

    def per_tile_pair(fn):
        def body(i, carry):
            for u in range(TILES_PER_TRIP):
                fn(TILES_PER_TRIP * i + u)
            return carry
        lax.fori_loop(0, nq // TILES_PER_TRIP, body, 0)

    def finalize(qtile):
        q0 = tile_start(qtile)
        a = acc[qtile]
        o2 = a[:V_HEAD_DIM] * (1.0 / a[V_HEAD_DIM:V_HEAD_DIM + 1])
        o = o2[:, :t] - lam * o2[:, t:]
        o = o * lax.rsqrt(jnp.mean(o * o, axis=0, keepdims=True) + EPS)
        o_ref[0, pl.ds(q0, t), :] = (o.T * gsub_ref[...]).astype(o_ref.dtype)

    @pl.when(fast_ref[0] == 1)
    def _():
        _attn_fast_sweep(t, nq, slope2, bound_ref[0], slope_pieces, pieces, stack_rows, build_wq,
                         finalize, kaug, vaug, wq, acc, dbias,
                         ((s2_even, p2_even), (s2_odd, p2_odd)))

    @pl.when(fast_ref[0] == 0)
    def _():
        _attn_safe_sweep(t, nq, slope2, slope_pieces, stack_rows, build_wq, per_tile_pair, kaug,
                         vaug, wq, acc, m_all, s_even, s_odd, p_even, p_odd, a_even, a_odd, dbias)
        per_tile_pair(finalize)


def _attn_fast_sweep(t, nq, slope2, bound, slope_pieces, pieces, stack_rows, build_wq, finalize,
                     kaug, vaug, wq, acc, dbias, bufs):
    tile_pieces = tuple(p * float(t) for p in slope_pieces)
    col = lax.broadcasted_iota(jnp.int32, (1, 2 * t), 1)
    col = jnp.where(col >= t, col - t, col)

    def position_rows(qtile):
        qpos = (qtile * t + col).astype(jnp.float32)
        return stack_rows(slope_pieces + tile_pieces + pieces(-(slope2 * qpos) - bound))

    steps = []
    for q in range(nq):
        steps += [(q, j, 2, False) for j in range(0, q - q % 2, 2)]
        steps.append((q, q, 1, True) if q % 2 == 0 else (q, q - 1, 2, True))
    first_step = {q: min(i for i, s in enumerate(steps) if s[0] == q) for q in range(nq)}
    last_step = {q: max(i for i, s in enumerate(steps) if s[0] == q) for q in range(nq)}

    def scores(i):
        q, j, ktiles, _ = steps[i]
        rows = ktiles * t
        bufs[i % 2][0][:rows] = jnp.dot(kaug[j * t:j * t + rows, :], wq[q],
                                        preferred_element_type=jnp.float32)

    def probabilities(i):
        _, _, ktiles, diag = steps[i]
        s_src, p_dst = bufs[i % 2]
        rows = ktiles * t
        below = rows - t if diag else rows
        if below:
            p_dst[:below] = jnp.exp2(s_src[:below]).astype(p_dst.dtype)
        if diag:
            p_dst[below:rows] = jnp.exp2(s_src[below:rows] + dbias[...]).astype(p_dst.dtype)

    def weighted_values(i):
        q, j, ktiles, _ = steps[i]
        rows = ktiles * t
        pv = jnp.dot(vaug[:, j * t:j * t + rows], bufs[i % 2][1][:rows],
                     preferred_element_type=jnp.float32)
        acc[q] = pv if i == first_step[q] else acc[q] + pv

    wq_rows = functools.partial(build_wq, position_rows=position_rows)
    wq_rows(0)
    wq_rows(1)
    scores(0)
    probabilities(0)
    scores(1)
    for i, (q, _, _, _) in enumerate(steps):
        if i == first_step[q] and q + 2 < nq:
            wq_rows(q + 2)
        if i + 2 < len(steps):
            scores(i + 2)
        if i + 1 < len(steps):
            probabilities(i + 1)
        weighted_values(i)
        if i == last_step[q]:
            finalize(q)


def _attn_safe_sweep(t, nq, slope2, slope_pieces, stack_rows, build_wq, per_tile_pair, kaug, vaug,
                     wq, acc, m_all, s_even, s_odd, p_even, p_odd, a_even, a_odd, dbias):
    only_offset_rows = stack_rows(slope_pieces)
    per_tile_pair(functools.partial(build_wq, position_rows=lambda qtile: only_offset_rows))

    def scores(qtile, ktile, s_dst):
        k0 = pl.multiple_of(ktile * t, t)
        s_dst[...] = jnp.dot(kaug[pl.ds(k0, t), :], wq[qtile],
                             preferred_element_type=jnp.float32)

    def softmax(qtile, ktile, s_src, p_dst, a_dst, diag):
        s = s_src[...]
        if diag:
            s = s + dbias[...]
            m_new = jnp.max(s, axis=0, keepdims=True)
            p_dst[...] = jnp.exp2(s - m_new).astype(p_dst.dtype)
        else:
            shift = slope2 * ((ktile - qtile) * t).astype(jnp.float32)
            m_old = m_all[qtile]
            m_new = jnp.maximum(m_old, jnp.max(s, axis=0, keepdims=True) + shift)
            p_dst[...] = jnp.exp2(s - (m_new - shift)).astype(p_dst.dtype)
            a_dst[...] = jnp.exp2(m_old - m_new)
        m_all[qtile] = m_new

    def weighted_values(qtile, ktile, p_src, a_src, diag):
        k0 = pl.multiple_of(ktile * t, t)
        pv = jnp.dot(vaug[:, pl.ds(k0, t)], p_src[...], preferred_element_type=jnp.float32)
        acc[qtile] = pv if diag else a_src[...] * acc[qtile] + pv

    bufs = ((s_even, p_even, a_even), (s_odd, p_odd, a_odd))

    spare = (jnp.int32(nq), jnp.int32(0))
    p_odd[...] = jnp.zeros_like(p_odd)
    a_odd[...] = jnp.ones_like(a_odd)
    acc[nq] = jnp.zeros(acc.shape[1:], acc.dtype)

    def run_pairs(first, advance, n_steps, unroll, diag):
        assert n_steps % unroll == 0 and unroll % 2 == 0
        scores(*first, s_even)

        def body(_, state):
            for k in range(unroll):
                (s_cur, p_cur, a_cur), (s_nxt, p_prev, a_prev) = bufs[k % 2], bufs[(k + 1) % 2]
                q, j, qp, jp = state
                qn, jn = advance(q, j)
                qn, jn = jnp.minimum(qn, nq - 1), jnp.minimum(jn, nq - 1)
                scores(qn, jn, s_nxt)
                softmax(q, j, s_cur, p_cur, a_cur, diag)
                weighted_values(qp, jp, p_prev, a_prev, diag)
                state = (qn, jn, q, j)
            return state

        state = lax.fori_loop(0, n_steps // unroll, body, (*first, *spare))
        weighted_values(state[2], state[3], p_odd, a_odd, diag)

    i32 = jnp.int32
    run_pairs((i32(0), i32(0)), lambda q, j: (q + 1, j + 1), nq, DIAG_UNROLL, True)

    def below_diag(q, j):
        wrap = j + 1 == q
        return jnp.where(wrap, q + 1, q), jnp.where(wrap, 0, j + 1)

    run_pairs((i32(1), i32(0)), below_diag, nq * (nq - 1) // 2, PAIR_UNROLL, False)


def _attn_call(slopes, lq, gq_col, gk_max, gsub_row, qt, kn, vt, lambda_init):
    batch, _, seq = qt.shape
    t = ATTN_TILE
    nq = seq // t
    assert nq % TILES_PER_TRIP == 0
    assert nq % DIAG_UNROLL == 0 and (nq * (nq - 1) // 2) % PAIR_UNROLL == 0
    assert t <= 256, "in-tile key offsets must be exact in bf16"
    kernel = functools.partial(_attn_kernel, seq=seq, lambda_init=lambda_init)
    f32, bf = jnp.float32, jnp.bfloat16
    bound = (BOUND_MARGIN * HEAD_DIM * jnp.max(jnp.abs(gq_col)) * gk_max).astype(f32).reshape(1)
    fast = (bound <= FAST_BOUND_MAX).astype(jnp.int32)
    return pl.pallas_call(
        kernel,
        grid=(batch, N_HEADS),
        in_specs=[
            pl.BlockSpec(memory_space=pltpu.SMEM),
            pl.BlockSpec(memory_space=pltpu.SMEM),
            pl.BlockSpec(memory_space=pltpu.SMEM),
            _const_spec(lq.shape),
            _const_spec(gq_col.shape),
            _const_spec(gsub_row.shape),
            pl.BlockSpec((1, 2 * HEAD_DIM, seq), lambda b, h: (b, h, 0)),
            pl.BlockSpec((1, seq, 2 * HEAD_DIM), lambda b, h: (b, 0, h)),
            pl.BlockSpec((1, V_HEAD_DIM, seq), lambda b, h: (b, h, 0)),
        ],
        out_specs=pl.BlockSpec((1, seq, V_HEAD_DIM), lambda b, h: (b, 0, h)),
        out_shape=jax.ShapeDtypeStruct((batch, seq, V_DIM), bf),
        scratch_shapes=[
            pltpu.VMEM((seq, K_AUG), bf),
            pltpu.VMEM((V_AUG, seq), bf),
            pltpu.VMEM((nq, K_AUG, 2 * t), bf),
            pltpu.VMEM((nq + 1, V_AUG, 2 * t), f32),
            pltpu.VMEM((nq, 1, 2 * t), f32),
            pltpu.VMEM((t, 2 * t), f32),
            pltpu.VMEM((t, 2 * t), f32),
            pltpu.VMEM((t, 2 * t), bf),
            pltpu.VMEM((t, 2 * t), bf),
            pltpu.VMEM((1, 2 * t), f32),
            pltpu.VMEM((1, 2 * t), f32),
            pltpu.VMEM((t, 2 * t), f32),
            pltpu.VMEM((2 * t, 2 * t), f32),
            pltpu.VMEM((2 * t, 2 * t), f32),
            pltpu.VMEM((2 * t, 2 * t), bf),
            pltpu.VMEM((2 * t, 2 * t), bf),
        ],
        compiler_params=pltpu.CompilerParams(
            dimension_semantics=("arbitrary", "arbitrary"), vmem_limit_bytes=VMEM_LIMIT_BYTES),
        name="attn",
    )(slopes, fast, bound, lq, gq_col, gsub_row, qt, kn, vt)


def _mix_kernel(x_ref, u_ref, halo_ref, ya_ref, gate_ref, wpg_ref, ps_ref, wbp_ref, wba_ref,
                wout_ref, gffn_ref, wup_ref, wdown_ref, o_ref, *, tiles_per_seq):
    tm = TOKEN_TILE
    i = pl.program_id(0)
    tile_in_seq = i % tiles_per_seq

    halo = jnp.where(tile_in_seq == 0, 0.0, halo_ref[...].astype(jnp.float32))
    u = u_ref[...].astype(jnp.float32)
    ext = jnp.concatenate([halo, u], axis=0)
    pos = tile_in_seq * tm + lax.broadcasted_iota(jnp.int32, (tm, POOL_GROUP_DIM), 0)
    win = ext
    ys = []
    for g, w in enumerate(POOL_WINDOWS):
        win = win[:, POOL_GROUP_DIM * (1 if g else 0):]
        win = win + pltpu.roll(win, w // 2, 0)
        cnt = jnp.minimum(pos + 1, w).astype(jnp.float32)
        ug = u[:, g * POOL_GROUP_DIM:(g + 1) * POOL_GROUP_DIM]
        mixed = win[POOL_HALO:, :POOL_GROUP_DIM] / cnt - ug
        ys.append(jnp.dot(mixed.astype(jnp.bfloat16), wpg_ref[g],
                          preferred_element_type=jnp.float32))
    y_pool = (jnp.concatenate(ys, axis=1) * ps_ref[...]).astype(jnp.bfloat16)

    bp = jnp.dot(y_pool, wbp_ref[...], preferred_element_type=jnp.float32)
    ba = jnp.dot(ya_ref[...], wba_ref[...], preferred_element_type=jnp.float32)
    gates = gate_ref[...].astype(jnp.float32)
    merged = gates[:, :D_MODEL] * bp + gates[:, D_MODEL:] * ba
    x1 = x_ref[...] + jnp.dot(merged.astype(jnp.bfloat16), wout_ref[...],
                              preferred_element_type=jnp.float32)

    ms = jnp.mean(x1 * x1, axis=-1, keepdims=True)
    h2 = (x1 * lax.rsqrt(ms + EPS) * gffn_ref[...]).astype(jnp.bfloat16)
    y = x1
    for cidx in range(D_FF // FF_CHUNK):
        sl = slice(cidx * FF_CHUNK, (cidx + 1) * FF_CHUNK)
        up = jnp.maximum(jnp.dot(h2, wup_ref[:, sl], preferred_element_type=jnp.float32), 0.0)
        y = y + jnp.dot((up * up).astype(jnp.bfloat16), wdown_ref[sl, :],
                        preferred_element_type=jnp.float32)
    o_ref[...] = y


def _mix_call(x2d, u, y_attn, gates, w_pg, pool_scale, w_bp, w_ba, w_out, g_ffn, w_up, w_down,
              seq):
    tokens = x2d.shape[0]
    tm = TOKEN_TILE
    row = lambda i: (i, 0)
    halo_blocks = tm // POOL_HALO
    kernel = functools.partial(_mix_kernel, tiles_per_seq=seq // tm)
    return pl.pallas_call(
        kernel,
        grid=(tokens // tm,),
        in_specs=[
            pl.BlockSpec((tm, D_MODEL), row),
            pl.BlockSpec((tm, POOL_DIM), row),
            pl.BlockSpec((POOL_HALO, POOL_DIM),
                         lambda i: (jnp.maximum(i * halo_blocks - 1, 0), 0)),
            pl.BlockSpec((tm, V_DIM), row),
            pl.BlockSpec((tm, 2 * D_MODEL), row),
            _const_spec(w_pg.shape),
            _const_spec((1, POOL_DIM)),
            _const_spec(w_bp.shape),
            _const_spec(w_ba.shape),
            _const_spec(w_out.shape),
            _const_spec((1, D_MODEL)),
            _const_spec(w_up.shape),
            _const_spec(w_down.shape),
        ],
        out_specs=pl.BlockSpec((tm, D_MODEL), row),
        out_shape=jax.ShapeDtypeStruct((tokens, D_MODEL), jnp.float32),
        compiler_params=pltpu.CompilerParams(
            dimension_semantics=("arbitrary",), vmem_limit_bytes=VMEM_LIMIT_BYTES),
        name="mix",
    )(x2d, u, u, y_attn, gates, w_pg, pool_scale, w_bp, w_ba, w_out, g_ffn, w_up, w_down)


def kernel(x, g_mix, w_in, w_pool_grp, pool_scale, g_q, g_k, lambda_qk, g_sub, w_branch_pool,
           w_branch_attn, w_gate, b_gate, w_out, g_ffn, w_up, w_down):
    batch, seq, d_model = x.shape
    depth = w_in.shape[0]
    assert d_model == D_MODEL and seq % TOKEN_TILE == 0 and seq % ATTN_TILE == 0
    assert TOKEN_TILE % POOL_HALO == 0 and POOL_HALO >= max(POOL_WINDOWS) - 1
    bf = jnp.bfloat16
    f32 = jnp.float32

    slopes = jnp.asarray([2.0 ** (-8.0 * (i + 1) / N_HEADS) for i in range(N_HEADS)], f32)
    grp = jnp.arange(MXU_COLS_V7X) // HEAD_DIM
    e_blk = (grp[:, None] == grp[None, :]).astype(bf)

    x2d = x.reshape(batch * seq, d_model)
    for l in range(depth):
        lambda_init = 0.8 - 0.6 * math.exp(-0.3 * l)
        k_lo, v_lo = POOL_DIM + QK_DIM, POOL_DIM + 2 * QK_DIM
        w_pk = jnp.concatenate([w_in[l][:, :POOL_DIM], w_in[l][:, k_lo:v_lo]], axis=1).astype(bf)
        w_qvt = jnp.concatenate([w_in[l][:, POOL_DIM:k_lo], w_in[l][:, v_lo:]], axis=1).T.astype(bf)
        gk_row = jnp.tile(g_k[l].astype(f32), 2 * N_HEADS).reshape(1, QK_DIM)
        gq_col = (g_q[l].astype(f32) * (HEAD_DIM ** -0.5 * LOG2E)).reshape(HEAD_DIM, 1)
        gsub_row = (g_sub[l].astype(f32) * (1.0 - lambda_init)).reshape(1, V_HEAD_DIM)

        u, kn, qt, vt, gates = _proj_call(
            x2d, g_mix[l].reshape(1, d_model).astype(f32), w_pk, w_qvt, w_gate[l].astype(bf),
            b_gate[l].reshape(1, 2 * d_model).astype(f32), gk_row, e_blk, batch, seq)
        y_attn = _attn_call(slopes, lambda_qk[l].astype(f32), gq_col,
                            jnp.max(jnp.abs(g_k[l].astype(f32))), gsub_row, qt,
                            kn.reshape(batch, seq, QK_DIM), vt, lambda_init)
        x2d = _mix_call(
            x2d, u, y_attn.reshape(batch * seq, V_DIM), gates, w_pool_grp[l].astype(bf),
            pool_scale[l].reshape(1, POOL_DIM).astype(f32), w_branch_pool[l].astype(bf),
            w_branch_attn[l].astype(bf), w_out[l].astype(bf),
            g_ffn[l].reshape(1, d_model).astype(f32), w_up[l].astype(bf), w_down[l].astype(bf), seq)
    return x2d.reshape(batch, seq, d_model)
```

```python
import functools
import math

import jax
import jax.numpy as jnp
from jax import lax
from jax.experimental import pallas as pl
from jax.experimental.pallas import tpu as pltpu

D_MODEL = 1024
CHUNK = 64
POOL_WINDOWS = (2, 4, 8, 16)
N_POOL_GROUPS = len(POOL_WINDOWS)
POOL_GROUP_DIM = 128
POOL_DIM = N_POOL_GROUPS * POOL_GROUP_DIM
N_HEADS = 8
HEAD_DIM = 64
V_HEAD_DIM = 2 * HEAD_DIM
QK_DIM = N_HEADS * 2 * HEAD_DIM
V_DIM = N_HEADS * V_HEAD_DIM
D_FF = 4 * D_MODEL
EPS = 1e-6
LOG2E = math.log2(math.e)

MXU_COLS_V7X = 256
BF16_ROWS_V7X = 16
VMEM_LIMIT_BYTES = 56 * 1024 * 1024

TOKEN_TILE = 512
ATTN_TILE = 256
TILES_PER_TRIP = 4
DIAG_UNROLL = 4
PAIR_UNROLL = 8
FAST_BOUND_MAX = 48.0
BOUND_MARGIN = 1.02
POOL_HALO = 16
FF_CHUNK = 1024
MASK_VALUE = -1e30

K_AUG = 2 * HEAD_DIM + MXU_COLS_V7X // 2
N_PIECES = 3
V_AUG = V_HEAD_DIM + BF16_ROWS_V7X


def _const_spec(shape):
    nd = len(shape)
    return pl.BlockSpec(shape, lambda *_: (0,) * nd, pipeline_mode=pl.Buffered(1))


def _proj_kernel(x_ref, g_ref, wpk_ref, wqvt_ref, wg_ref, bg_ref, gk_ref, e_ref,
                 u_ref, kn_ref, qt_ref, vt_ref, gate_ref):
    xf = x_ref[...]
    ms = jnp.mean(xf * xf, axis=-1, keepdims=True)
    h = (xf * lax.rsqrt(ms + EPS) * g_ref[...]).astype(jnp.bfloat16)

    zpk = jnp.dot(h, wpk_ref[...], preferred_element_type=jnp.float32)
    u_ref[...] = zpk[:, :POOL_DIM].astype(u_ref.dtype)

    zk = zpk[:, POOL_DIM:]
    sq = (zk * zk).astype(jnp.bfloat16)
    e = e_ref[...]
    ssq = jnp.concatenate(
        [jnp.dot(sq[:, c * MXU_COLS_V7X:(c + 1) * MXU_COLS_V7X], e, preferred_element_type=jnp.float32)
         for c in range(QK_DIM // MXU_COLS_V7X)], axis=1)
    kn = zk * lax.rsqrt(ssq * (1.0 / HEAD_DIM) + EPS) * gk_ref[...]
    kn_ref[...] = kn.astype(kn_ref.dtype)

    qvt = lax.dot_general(wqvt_ref[...], h, (((1,), (1,)), ((), ())),
                          preferred_element_type=jnp.float32)
    qt_ref[0] = qvt[:QK_DIM].astype(qt_ref.dtype)
    vt_ref[0] = qvt[QK_DIM:].astype(vt_ref.dtype)

    zg = jnp.dot(h, wg_ref[...], preferred_element_type=jnp.float32) + bg_ref[...]
    gate_ref[...] = (1.0 / (1.0 + jnp.exp(-zg))).astype(gate_ref.dtype)


def _proj_call(x2d, g_mix, w_pk, w_qvt, w_gate, b_gate, gk_row, e_blk, batch, seq):
    tokens = x2d.shape[0]
    tm = TOKEN_TILE
    nst = seq // tm
    bf = jnp.bfloat16
    row = lambda i: (i, 0)
    tr = lambda i: (i // nst, 0, i % nst)
    return pl.pallas_call(
        _proj_kernel,
        grid=(tokens // tm,),
        in_specs=[
            pl.BlockSpec((tm, D_MODEL), row),
            _const_spec((1, D_MODEL)),
            _const_spec(w_pk.shape),
            _const_spec(w_qvt.shape),
            _const_spec(w_gate.shape),
            _const_spec((1, 2 * D_MODEL)),
            _const_spec((1, QK_DIM)),
            _const_spec(e_blk.shape),
        ],
        out_specs=[
            pl.BlockSpec((tm, POOL_DIM), row),
            pl.BlockSpec((tm, QK_DIM), row),
            pl.BlockSpec((1, QK_DIM, tm), tr),
            pl.BlockSpec((1, V_DIM, tm), tr),
            pl.BlockSpec((tm, 2 * D_MODEL), row),
        ],
        out_shape=[
            jax.ShapeDtypeStruct((tokens, POOL_DIM), bf),
            jax.ShapeDtypeStruct((tokens, QK_DIM), bf),
            jax.ShapeDtypeStruct((batch, QK_DIM, seq), bf),
            jax.ShapeDtypeStruct((batch, V_DIM, seq), bf),
            jax.ShapeDtypeStruct((tokens, 2 * D_MODEL), bf),
        ],
        compiler_params=pltpu.CompilerParams(
            dimension_semantics=("arbitrary",), vmem_limit_bytes=VMEM_LIMIT_BYTES),
        name="proj",
    )(x2d, g_mix, w_pk, w_qvt, w_gate, b_gate, gk_row, e_blk)


def _attn_kernel(slopes_ref, fast_ref, bound_ref, lq_ref, gq_ref, gsub_ref, qt_ref, k_ref, vt_ref,
                 o_ref, kaug, vaug, wq, acc, m_all, s_even, s_odd, p_even, p_odd, a_even, a_odd,
                 dbias, s2_even, s2_odd, p2_even, p2_odd, *, seq, lambda_init):
    t = ATTN_TILE
    nq = seq // t
    slope2 = slopes_ref[pl.program_id(1)] * LOG2E
    n_extra = K_AUG - 2 * HEAD_DIM

    kaug[:, :2 * HEAD_DIM] = k_ref[0]
    row = lax.broadcasted_iota(jnp.int32, (t, n_extra), 0).astype(jnp.float32)
    lane = lax.broadcasted_iota(jnp.int32, (t, n_extra), 1)
    for i in range(nq):
        cols = jnp.where(lane < N_PIECES, row,
                         jnp.where(lane < 2 * N_PIECES, float(i),
                                   jnp.where(lane < 3 * N_PIECES, 1.0, 0.0)))
        kaug[i * t:(i + 1) * t, 2 * HEAD_DIM:] = cols.astype(kaug.dtype)
    vaug[:V_HEAD_DIM, :] = vt_ref[0]
    vaug[V_HEAD_DIM:, :] = jnp.ones((V_AUG - V_HEAD_DIM, seq), vaug.dtype)

    def pieces(x):
        hi = x.astype(jnp.bfloat16).astype(jnp.float32)
        mid = (x - hi).astype(jnp.bfloat16).astype(jnp.float32)
        return hi, mid, x - hi - mid

    def stack_rows(groups):
        prow = lax.broadcasted_iota(jnp.int32, (BF16_ROWS_V7X, 2 * t), 0)
        out = jnp.zeros((BF16_ROWS_V7X, 2 * t), jnp.float32)
        for idx, rowv in enumerate(groups):
            out = jnp.where(prow == idx, rowv, out)
        return out.astype(wq.dtype)

    slope_pieces = pieces(jnp.full((1, 2 * t), slope2, jnp.float32))

    r = lax.broadcasted_iota(jnp.int32, (t, t), 0)
    c = lax.broadcasted_iota(jnp.int32, (t, t), 1)
    ahead = slope2 * (2.0 * (c - r).astype(jnp.float32))
    one_map = jnp.where((r // CHUNK) <= (c // CHUNK), jnp.where(r <= c, 0.0, ahead), MASK_VALUE)
    dbias[:, :t] = one_map
    dbias[:, t:] = one_map

    lq = lq_ref[...]
    lam = (jnp.exp(jnp.sum(lq[0:1] * lq[1:2], axis=1, keepdims=True))
           - jnp.exp(jnp.sum(lq[2:3] * lq[3:4], axis=1, keepdims=True)) + lambda_init)
    gq = gq_ref[...]

    def tile_start(tile):
        return tile * t if isinstance(tile, int) else pl.multiple_of(tile * t, t)

    def build_wq(qtile, position_rows):
        q0 = tile_start(qtile)
        qt = qt_ref[0, :, pl.ds(q0, t)].astype(jnp.float32)
        zeros = jnp.zeros((HEAD_DIM, t), jnp.float32)
        for half in range(2):
            qh = qt[half * HEAD_DIM:(half + 1) * HEAD_DIM]
            qh = qh * lax.rsqrt(jnp.mean(qh * qh, axis=0, keepdims=True) + EPS) * gq
            both = jnp.concatenate([zeros, qh] if half else [qh, zeros], axis=1)
            wq[qtile, half * HEAD_DIM:(half + 1) * HEAD_DIM, :] = both.astype(wq.dtype)
        wq[qtile, 2 * HEAD_DIM:2 * HEAD_DIM + BF16_ROWS_V7X, :] = position_rows(qtile)
        wq[qtile, 2 * HEAD_DIM + BF16_ROWS_V7X:, :] = jnp.zeros(
            (n_extra - BF16_ROWS_V7X, 2 * t), wq.dtype)

    def per_tile_pair(fn):
        def body(i, carry):
            for u in range(TILES_PER_TRIP):
                fn(TILES_PER_TRIP * i + u)
            return carry
        lax.fori_loop(0, nq // TILES_PER_TRIP, body, 0)

    def finalize(qtile, denominator=None):
        q0 = tile_start(qtile)
        if denominator is None:
            denominator = acc[qtile, V_HEAD_DIM:V_HEAD_DIM + 1]
        o2 = acc[qtile, :V_HEAD_DIM] * (1.0 / denominator)
        o = o2[:, :t] - lam * o2[:, t:]
        o = o * lax.rsqrt(jnp.mean(o * o, axis=0, keepdims=True) + EPS)
        o_ref[0, pl.ds(q0, t), :] = (o.T * gsub_ref[...]).astype(o_ref.dtype)

    @pl.when(fast_ref[0] == 1)
    def _():
        _attn_fast_sweep(t, nq, slope2, bound_ref[0], slope_pieces, pieces, stack_rows, build_wq,
                         finalize, kaug, vaug, wq, acc, m_all, dbias,
                         ((s2_even, p2_even), (s2_odd, p2_odd)))

    @pl.when(fast_ref[0] == 0)
    def _():
        _attn_safe_sweep(t, nq, slope2, slope_pieces, stack_rows, build_wq, per_tile_pair, kaug,
                         vaug, wq, acc, m_all, s_even, s_odd, p_even, p_odd, a_even, a_odd, dbias)
        per_tile_pair(finalize)


def _attn_fast_sweep(t, nq, slope2, bound, slope_pieces, pieces, stack_rows, build_wq, finalize,
                     kaug, vaug, wq, acc, denom, dbias, bufs):
    tile_pieces = tuple(p * float(t) for p in slope_pieces)
    col = lax.broadcasted_iota(jnp.int32, (1, 2 * t), 1)
    col = jnp.where(col >= t, col - t, col)

    def position_rows(qtile):
        qpos = (qtile * t + col).astype(jnp.float32)
        return stack_rows(slope_pieces + tile_pieces + pieces(-(slope2 * qpos) - bound))

    steps = []
    for q in range(nq):
        steps += [(q, j, 2, False) for j in range(0, q - q % 2, 2)]
        steps.append((q, q, 1, True) if q % 2 == 0 else (q, q - 1, 2, True))
    first_step = {q: min(i for i, s in enumerate(steps) if s[0] == q) for q in range(nq)}
    last_step = {q: max(i for i, s in enumerate(steps) if s[0] == q) for q in range(nq)}

    def scores(i):
        q, j, ktiles, _ = steps[i]
        rows = ktiles * t
        bufs[i % 2][0][:rows] = jnp.dot(kaug[j * t:j * t + rows, :], wq[q],
                                        preferred_element_type=jnp.float32)

    def probabilities(i):
        q, _, ktiles, diag = steps[i]
        s_src, p_dst = bufs[i % 2]
        rows = ktiles * t
        below = rows - t if diag else rows
        total = None
        if below:
            p = jnp.exp2(s_src[:below])
            p_dst[:below] = p.astype(p_dst.dtype)
            total = jnp.sum(p, axis=0, keepdims=True)
        if diag:
            p = jnp.exp2(s_src[below:rows] + dbias[...])
            p_dst[below:rows] = p.astype(p_dst.dtype)
            part = jnp.sum(p, axis=0, keepdims=True)
            total = part if total is None else total + part
        denom[q] = total if i == first_step[q] else denom[q] + total

    def weighted_values(i):
        q, j, ktiles, _ = steps[i]
        rows = ktiles * t
        pv = jnp.dot(vaug[:V_HEAD_DIM, j * t:j * t + rows], bufs[i % 2][1][:rows],
                     preferred_element_type=jnp.float32)
        acc[q, :V_HEAD_DIM] = pv if i == first_step[q] else acc[q, :V_HEAD_DIM] + pv

    wq_rows = functools.partial(build_wq, position_rows=position_rows)
    wq_rows(0)
    wq_rows(1)
    scores(0)
    probabilities(0)
    scores(1)
    for i, (q, _, _, _) in enumerate(steps):
        if i == first_step[q] and q + 2 < nq:
            wq_rows(q + 2)
        if i + 2 < len(steps):
            scores(i + 2)
        if i + 1 < len(steps):
            probabilities(i + 1)
        weighted_values(i)
        if i == last_step[q]:
            finalize(q, denom[q])


def _attn_safe_sweep(t, nq, slope2, slope_pieces, stack_rows, build_wq, per_tile_pair, kaug, vaug,
                     wq, acc, m_all, s_even, s_odd, p_even, p_odd, a_even, a_odd, dbias):
    only_offset_rows = stack_rows(slope_pieces)
    per_tile_pair(functools.partial(build_wq, position_rows=lambda qtile: only_offset_rows))

    def scores(qtile, ktile, s_dst):
        k0 = pl.multiple_of(ktile * t, t)
        s_dst[...] = jnp.dot(kaug[pl.ds(k0, t), :], wq[qtile],
                             preferred_element_type=jnp.float32)

    def softmax(qtile, ktile, s_src, p_dst, a_dst, diag):
        s = s_src[...]
        if diag:
            s = s + dbias[...]
            m_new = jnp.max(s, axis=0, keepdims=True)
            p_dst[...] = jnp.exp2(s - m_new).astype(p_dst.dtype)
        else:
            shift = slope2 * ((ktile - qtile) * t).astype(jnp.float32)
            m_old = m_all[qtile]
            m_new = jnp.maximum(m_old, jnp.max(s, axis=0, keepdims=True) + shift)
            p_dst[...] = jnp.exp2(s - (m_new - shift)).astype(p_dst.dtype)
            a_dst[...] = jnp.exp2(m_old - m_new)
        m_all[qtile] = m_new

    def weighted_values(qtile, ktile, p_src, a_src, diag):
        k0 = pl.multiple_of(ktile * t, t)
        pv = jnp.dot(vaug[:, pl.ds(k0, t)], p_src[...], preferred_element_type=jnp.float32)
        acc[qtile] = pv if diag else a_src[...] * acc[qtile] + pv

    bufs = ((s_even, p_even, a_even), (s_odd, p_odd, a_odd))

    spare = (jnp.int32(nq), jnp.int32(0))
    p_odd[...] = jnp.zeros_like(p_odd)
    a_odd[...] = jnp.ones_like(a_odd)
    acc[nq] = jnp.zeros(acc.shape[1:], acc.dtype)

    def run_pairs(first, advance, n_steps, unroll, diag):
        assert n_steps % unroll == 0 and unroll % 2 == 0
        scores(*first, s_even)

        def body(_, state):
            for k in range(unroll):
                (s_cur, p_cur, a_cur), (s_nxt, p_prev, a_prev) = bufs[k % 2], bufs[(k + 1) % 2]
                q, j, qp, jp = state
                qn, jn = advance(q, j)
                qn, jn = jnp.minimum(qn, nq - 1), jnp.minimum(jn, nq - 1)
                scores(qn, jn, s_nxt)
                softmax(q, j, s_cur, p_cur, a_cur, diag)
                weighted_values(qp, jp, p_prev, a_prev, diag)
                state = (qn, jn, q, j)
            return state

        state = lax.fori_loop(0, n_steps // unroll, body, (*first, *spare))
        weighted_values(state[2], state[3], p_odd, a_odd, diag)

    i32 = jnp.int32
    run_pairs((i32(0), i32(0)), lambda q, j: (q + 1, j + 1), nq, DIAG_UNROLL, True)

    def below_diag(q, j):
        wrap = j + 1 == q
        return jnp.where(wrap, q + 1, q), jnp.where(wrap, 0, j + 1)

    run_pairs((i32(1), i32(0)), below_diag, nq * (nq - 1) // 2, PAIR_UNROLL, False)


def _attn_call(slopes, lq, gq_col, gk_max, gsub_row, qt, kn, vt, lambda_init):
    batch, _, seq = qt.shape
    t = ATTN_TILE
    nq = seq // t
    assert nq % TILES_PER_TRIP == 0
    assert nq % DIAG_UNROLL == 0 and (nq * (nq - 1) // 2) % PAIR_UNROLL == 0
    assert t <= 256, "in-tile key offsets must be exact in bf16"
    kernel = functools.partial(_attn_kernel, seq=seq, lambda_init=lambda_init)
    f32, bf = jnp.float32, jnp.bfloat16
    bound = (BOUND_MARGIN * HEAD_DIM * jnp.max(jnp.abs(gq_col)) * gk_max).astype(f32).reshape(1)
    fast = (bound <= FAST_BOUND_MAX).astype(jnp.int32)
    return pl.pallas_call(
        kernel,
        grid=(batch, N_HEADS),
        in_specs=[
            pl.BlockSpec(memory_space=pltpu.SMEM),
            pl.BlockSpec(memory_space=pltpu.SMEM),
            pl.BlockSpec(memory_space=pltpu.SMEM),
            _const_spec(lq.shape),
            _const_spec(gq_col.shape),
            _const_spec(gsub_row.shape),
            pl.BlockSpec((1, 2 * HEAD_DIM, seq), lambda b, h: (b, h, 0)),
            pl.BlockSpec((1, seq, 2 * HEAD_DIM), lambda b, h: (b, 0, h)),
            pl.BlockSpec((1, V_HEAD_DIM, seq), lambda b, h: (b, h, 0)),
        ],
        out_specs=pl.BlockSpec((1, seq, V_HEAD_DIM), lambda b, h: (b, 0, h)),
        out_shape=jax.ShapeDtypeStruct((batch, seq, V_DIM), bf),
        scratch_shapes=[
            pltpu.VMEM((seq, K_AUG), bf),
            pltpu.VMEM((V_AUG, seq), bf),
            pltpu.VMEM((nq, K_AUG, 2 * t), bf),
            pltpu.VMEM((nq + 1, V_AUG, 2 * t), f32),
            pltpu.VMEM((nq, 1, 2 * t), f32),
            pltpu.VMEM((t, 2 * t), f32),
            pltpu.VMEM((t, 2 * t), f32),
            pltpu.VMEM((t, 2 * t), bf),
            pltpu.VMEM((t, 2 * t), bf),
            pltpu.VMEM((1, 2 * t), f32),
            pltpu.VMEM((1, 2 * t), f32),
            pltpu.VMEM((t, 2 * t), f32),
            pltpu.VMEM((2 * t, 2 * t), f32),
            pltpu.VMEM((2 * t, 2 * t), f32),
            pltpu.VMEM((2 * t, 2 * t), bf),
            pltpu.VMEM((2 * t, 2 * t), bf),
        ],
        compiler_params=pltpu.CompilerParams(
            dimension_semantics=("arbitrary", "arbitrary"), vmem_limit_bytes=VMEM_LIMIT_BYTES),
        name="attn",
    )(slopes, fast, bound, lq, gq_col, gsub_row, qt, kn, vt)


def _mix_kernel(x_ref, u_ref, halo_ref, ya_ref, gate_ref, wpg_ref, ps_ref, wbp_ref, wba_ref,
                wout_ref, gffn_ref, wup_ref, wdown_ref, o_ref, *, tiles_per_seq):
    tm = TOKEN_TILE
    i = pl.program_id(0)
    tile_in_seq = i % tiles_per_seq

    halo = jnp.where(tile_in_seq == 0, 0.0, halo_ref[...].astype(jnp.float32))
    u = u_ref[...].astype(jnp.float32)
    ext = jnp.concatenate([halo, u], axis=0)
    pos = tile_in_seq * tm + lax.broadcasted_iota(jnp.int32, (tm, POOL_GROUP_DIM), 0)
    win = ext
    ys = []
    for g, w in enumerate(POOL_WINDOWS):
        win = win[:, POOL_GROUP_DIM * (1 if g else 0):]
        win = win + pltpu.roll(win, w // 2, 0)
        cnt = jnp.minimum(pos + 1, w).astype(jnp.float32)
        ug = u[:, g * POOL_GROUP_DIM:(g + 1) * POOL_GROUP_DIM]
        mixed = win[POOL_HALO:, :POOL_GROUP_DIM] / cnt - ug
        ys.append(jnp.dot(mixed.astype(jnp.bfloat16), wpg_ref[g],
                          preferred_element_type=jnp.float32))
    y_pool = (jnp.concatenate(ys, axis=1) * ps_ref[...]).astype(jnp.bfloat16)

    bp = jnp.dot(y_pool, wbp_ref[...], preferred_element_type=jnp.float32)
    ba = jnp.dot(ya_ref[...], wba_ref[...], preferred_element_type=jnp.float32)
    gates = gate_ref[...].astype(jnp.float32)
    merged = gates[:, :D_MODEL] * bp + gates[:, D_MODEL:] * ba
    x1 = x_ref[...] + jnp.dot(merged.astype(jnp.bfloat16), wout_ref[...],
                              preferred_element_type=jnp.float32)

    ms = jnp.mean(x1 * x1, axis=-1, keepdims=True)
    h2 = (x1 * lax.rsqrt(ms + EPS) * gffn_ref[...]).astype(jnp.bfloat16)
    y = x1
    for cidx in range(D_FF // FF_CHUNK):
        sl = slice(cidx * FF_CHUNK, (cidx + 1) * FF_CHUNK)
        up = jnp.maximum(jnp.dot(h2, wup_ref[:, sl], preferred_element_type=jnp.float32), 0.0)
        y = y + jnp.dot((up * up).astype(jnp.bfloat16), wdown_ref[sl, :],
                        preferred_element_type=jnp.float32)
    o_ref[...] = y


def _mix_call(x2d, u, y_attn, gates, w_pg, pool_scale, w_bp, w_ba, w_out, g_ffn, w_up, w_down,
              seq):
    tokens = x2d.shape[0]
    tm = TOKEN_TILE
    row = lambda i: (i, 0)
    halo_blocks = tm // POOL_HALO
    kernel = functools.partial(_mix_kernel, tiles_per_seq=seq // tm)
    return pl.pallas_call(
        kernel,
        grid=(tokens // tm,),
        in_specs=[
            pl.BlockSpec((tm, D_MODEL), row),
            pl.BlockSpec((tm, POOL_DIM), row),
            pl.BlockSpec((POOL_HALO, POOL_DIM),
                         lambda i: (jnp.maximum(i * halo_blocks - 1, 0), 0)),
            pl.BlockSpec((tm, V_DIM), row),
            pl.BlockSpec((tm, 2 * D_MODEL), row),
            _const_spec(w_pg.shape),
            _const_spec((1, POOL_DIM)),
            _const_spec(w_bp.shape),
            _const_spec(w_ba.shape),
            _const_spec(w_out.shape),
            _const_spec((1, D_MODEL)),
            _const_spec(w_up.shape),
            _const_spec(w_down.shape),
        ],
        out_specs=pl.BlockSpec((tm, D_MODEL), row),
        out_shape=jax.ShapeDtypeStruct((tokens, D_MODEL), jnp.float32),
        compiler_params=pltpu.CompilerParams(
            dimension_semantics=("arbitrary",), vmem_limit_bytes=VMEM_LIMIT_BYTES),
        name="mix",
    )(x2d, u, u, y_attn, gates, w_pg, pool_scale, w_bp, w_ba, w_out, g_ffn, w_up, w_down)


def kernel(x, g_mix, w_in, w_pool_grp, pool_scale, g_q, g_k, lambda_qk, g_sub, w_branch_pool,
           w_branch_attn, w_gate, b_gate, w_out, g_ffn, w_up, w_down):
    batch, seq, d_model = x.shape
    depth = w_in.shape[0]
    assert d_model == D_MODEL and seq % TOKEN_TILE == 0 and seq % ATTN_TILE == 0
    assert TOKEN_TILE % POOL_HALO == 0 and POOL_HALO >= max(POOL_WINDOWS) - 1
    bf = jnp.bfloat16
    f32 = jnp.float32

    slopes = jnp.asarray([2.0 ** (-8.0 * (i + 1) / N_HEADS) for i in range(N_HEADS)], f32)
    grp = jnp.arange(MXU_COLS_V7X) // HEAD_DIM
    e_blk = (grp[:, None] == grp[None, :]).astype(bf)

    x2d = x.reshape(batch * seq, d_model)
    for l in range(depth):
        lambda_init = 0.8 - 0.6 * math.exp(-0.3 * l)
        k_lo, v_lo = POOL_DIM + QK_DIM, POOL_DIM + 2 * QK_DIM
        w_pk = jnp.concatenate([w_in[l][:, :POOL_DIM], w_in[l][:, k_lo:v_lo]], axis=1).astype(bf)
        w_qvt = jnp.concatenate([w_in[l][:, POOL_DIM:k_lo], w_in[l][:, v_lo:]], axis=1).T.astype(bf)
        gk_row = jnp.tile(g_k[l].astype(f32), 2 * N_HEADS).reshape(1, QK_DIM)
        gq_col = (g_q[l].astype(f32) * (HEAD_DIM ** -0.5 * LOG2E)).reshape(HEAD_DIM, 1)
        gsub_row = (g_sub[l].astype(f32) * (1.0 - lambda_init)).reshape(1, V_HEAD_DIM)

        u, kn, qt, vt, gates = _proj_call(
            x2d, g_mix[l].reshape(1, d_model).astype(f32), w_pk, w_qvt, w_gate[l].astype(bf),
            b_gate[l].reshape(1, 2 * d_model).astype(f32), gk_row, e_blk, batch, seq)
        y_attn = _attn_call(slopes, lambda_qk[l].astype(f32), gq_col,
                            jnp.max(jnp.abs(g_k[l].astype(f32))), gsub_row, qt,
                            kn.reshape(batch, seq, QK_DIM), vt, lambda_init)
        x2d = _mix_call(
            x2d, u, y_attn.reshape(batch * seq, V_DIM), gates, w_pool_grp[l].astype(bf),
            pool_scale[l].reshape(1, POOL_DIM).astype(f32), w_branch_pool[l].astype(bf),
            w_branch_attn[l].astype(bf), w_out[l].astype(bf),
            g_ffn[l].reshape(1, d_model).astype(f32), w_up[l].astype(bf), w_down[l].astype(bf), seq)
    return x2d.reshape(batch, seq, d_model)
```

```python
import functools
import math

import jax
import jax.numpy as jnp
from jax import lax
from jax.experimental import pallas as pl
from jax.experimental.pallas import tpu as pltpu

D_MODEL = 1024
CHUNK = 64
POOL_WINDOWS = (2, 4, 8, 16)
N_POOL_GROUPS = len(POOL_WINDOWS)
POOL_GROUP_DIM = 128
POOL_DIM = N_POOL_GROUPS * POOL_GROUP_DIM
N_HEADS = 8
HEAD_DIM = 64
V_HEAD_DIM = 2 * HEAD_DIM
QK_DIM = N_HEADS * 2 * HEAD_DIM
V_DIM = N_HEADS * V_HEAD_DIM
D_FF = 4 * D_MODEL
EPS = 1e-6
LOG2E = math.log2(math.e)

MXU_COLS_V7X = 256
BF16_ROWS_V7X = 16
VMEM_LIMIT_BYTES = 56 * 1024 * 1024

TOKEN_TILE = 512
ATTN_TILE = 256
TILES_PER_TRIP = 4
DIAG_UNROLL = 4
PAIR_UNROLL = 8
FAST_BOUND_MAX = 48.0
BOUND_MARGIN = 1.02
POOL_HALO = 16
FF_CHUNK = 1024
MASK_VALUE = -1e30

K_AUG = 2 * HEAD_DIM + MXU_COLS_V7X // 2
N_PIECES = 3
V_AUG = V_HEAD_DIM + BF16_ROWS_V7X


def _const_spec(shape):
    nd = len(shape)
    return pl.BlockSpec(shape, lambda *_: (0,) * nd, pipeline_mode=pl.Buffered(1))


def _layer_spec(stacked, layer):
    rest = stacked.shape[1:]
    return pl.BlockSpec((None,) + rest, lambda *_: (layer,) + (0,) * len(rest),
                        pipeline_mode=pl.Buffered(1))


def _proj_kernel(x_ref, g_ref, wpk_ref, wqvt_ref, wg_ref, bg_ref, gk_ref, e_ref,
                 u_ref, kn_ref, qt_ref, vt_ref, gate_ref):
    xf = x_ref[...]
    ms = jnp.mean(xf * xf, axis=-1, keepdims=True)
    h = (xf * lax.rsqrt(ms + EPS) * g_ref[...]).astype(jnp.bfloat16)

    zpk = jnp.dot(h, wpk_ref[...], preferred_element_type=jnp.float32)
    u_ref[...] = zpk[:, :POOL_DIM].astype(u_ref.dtype)

    zk = zpk[:, POOL_DIM:]
    sq = (zk * zk).astype(jnp.bfloat16)
    e = e_ref[...]
    ssq = jnp.concatenate(
        [jnp.dot(sq[:, c * MXU_COLS_V7X:(c + 1) * MXU_COLS_V7X], e, preferred_element_type=jnp.float32)
         for c in range(QK_DIM // MXU_COLS_V7X)], axis=1)
    kn = zk * lax.rsqrt(ssq * (1.0 / HEAD_DIM) + EPS) * gk_ref[...]
    kn_ref[...] = kn.astype(kn_ref.dtype)

    qvt = lax.dot_general(wqvt_ref[...], h, (((1,), (1,)), ((), ())),
                          preferred_element_type=jnp.float32)
    qt_ref[0] = qvt[:QK_DIM].astype(qt_ref.dtype)
    vt_ref[0] = qvt[QK_DIM:].astype(vt_ref.dtype)

    zg = jnp.dot(h, wg_ref[...], preferred_element_type=jnp.float32) + bg_ref[...]
    gate_ref[...] = (1.0 / (1.0 + jnp.exp(-zg))).astype(gate_ref.dtype)


def _proj_call(layer, x2d, g_mix, w_pk, w_qvt, w_gate, b_gate, gk_row, e_blk, batch, seq):
    tokens = x2d.shape[0]
    tm = TOKEN_TILE
    nst = seq // tm
    bf = jnp.bfloat16
    row = lambda i: (i, 0)
    tr = lambda i: (i // nst, 0, i % nst)
    return pl.pallas_call(
        _proj_kernel,
        grid=(tokens // tm,),
        in_specs=[
            pl.BlockSpec((tm, D_MODEL), row),
            _layer_spec(g_mix, layer),
            _layer_spec(w_pk, layer),
            _layer_spec(w_qvt, layer),
            _layer_spec(w_gate, layer),
            _layer_spec(b_gate, layer),
            _layer_spec(gk_row, layer),
            _const_spec(e_blk.shape),
        ],
        out_specs=[
            pl.BlockSpec((tm, POOL_DIM), row),
            pl.BlockSpec((tm, QK_DIM), row),
            pl.BlockSpec((1, QK_DIM, tm), tr),
            pl.BlockSpec((1, V_DIM, tm), tr),
            pl.BlockSpec((tm, 2 * D_MODEL), row),
        ],
        out_shape=[
            jax.ShapeDtypeStruct((tokens, POOL_DIM), bf),
            jax.ShapeDtypeStruct((tokens, QK_DIM), bf),
            jax.ShapeDtypeStruct((batch, QK_DIM, seq), bf),
            jax.ShapeDtypeStruct((batch, V_DIM, seq), bf),
            jax.ShapeDtypeStruct((tokens, 2 * D_MODEL), bf),
        ],
        compiler_params=pltpu.CompilerParams(
            dimension_semantics=("arbitrary",), vmem_limit_bytes=VMEM_LIMIT_BYTES),
        name="proj",
    )(x2d, g_mix, w_pk, w_qvt, w_gate, b_gate, gk_row, e_blk)


def _attn_kernel(slopes_ref, fast_ref, bound_ref, lq_ref, gq_ref, gsub_ref, qt_ref, k_ref, vt_ref,
                 o_ref, kaug, vaug, wq, acc, m_all, s_even, s_odd, p_even, p_odd, a_even, a_odd,
                 dbias, s2_even, s2_odd, p2_even, p2_odd, *, seq, lambda_init, layer):
    t = ATTN_TILE
    nq = seq // t
    slope2 = slopes_ref[pl.program_id(1)] * LOG2E
    n_extra = K_AUG - 2 * HEAD_DIM

    kaug[:, :2 * HEAD_DIM] = k_ref[0]
    row = lax.broadcasted_iota(jnp.int32, (t, n_extra), 0).astype(jnp.float32)
    lane = lax.broadcasted_iota(jnp.int32, (t, n_extra), 1)
    for i in range(nq):
        cols = jnp.where(lane < N_PIECES, row,
                         jnp.where(lane < 2 * N_PIECES, float(i),
                                   jnp.where(lane < 3 * N_PIECES, 1.0, 0.0)))
        kaug[i * t:(i + 1) * t, 2 * HEAD_DIM:] = cols.astype(kaug.dtype)
    vaug[:V_HEAD_DIM, :] = vt_ref[0]
    vaug[V_HEAD_DIM:, :] = jnp.ones((V_AUG - V_HEAD_DIM, seq), vaug.dtype)

    def pieces(x):
        hi = x.astype(jnp.bfloat16).astype(jnp.float32)
        mid = (x - hi).astype(jnp.bfloat16).astype(jnp.float32)
        return hi, mid, x - hi - mid

    def stack_rows(groups):
        prow = lax.broadcasted_iota(jnp.int32, (BF16_ROWS_V7X, 2 * t), 0)
        out = jnp.zeros((BF16_ROWS_V7X, 2 * t), jnp.float32)
        for idx, rowv in enumerate(groups):
            out = jnp.where(prow == idx, rowv, out)
        return out.astype(wq.dtype)

    slope_pieces = pieces(jnp.full((1, 2 * t), slope2, jnp.float32))

    r = lax.broadcasted_iota(jnp.int32, (t, t), 0)
    c = lax.broadcasted_iota(jnp.int32, (t, t), 1)
    ahead = slope2 * (2.0 * (c - r).astype(jnp.float32))
    one_map = jnp.where((r // CHUNK) <= (c // CHUNK), jnp.where(r <= c, 0.0, ahead), MASK_VALUE)
    dbias[:, :t] = one_map
    dbias[:, t:] = one_map

    lq = lq_ref[...]
    lam = (jnp.exp(jnp.sum(lq[0:1] * lq[1:2], axis=1, keepdims=True))
           - jnp.exp(jnp.sum(lq[2:3] * lq[3:4], axis=1, keepdims=True)) + lambda_init)
    gq = gq_ref[...]

    def tile_start(tile):
        return tile * t if isinstance(tile, int) else pl.multiple_of(tile * t, t)

    def build_wq(qtile, position_rows):
        q0 = tile_start(qtile)
        qt = qt_ref[0, :, pl.ds(q0, t)].astype(jnp.float32)
        zeros = jnp.zeros((HEAD_DIM, t), jnp.float32)
        for half in range(2):
            qh = qt[half * HEAD_DIM:(half + 1) * HEAD_DIM]
            qh = qh * lax.rsqrt(jnp.mean(qh * qh, axis=0, keepdims=True) + EPS) * gq
            both = jnp.concatenate([zeros, qh] if half else [qh, zeros], axis=1)
            wq[qtile, half * HEAD_DIM:(half + 1) * HEAD_DIM, :] = both.astype(wq.dtype)
        wq[qtile, 2 * HEAD_DIM:2 * HEAD_DIM + BF16_ROWS_V7X, :] = position_rows(qtile)
        wq[qtile, 2 * HEAD_DIM + BF16_ROWS_V7X:, :] = jnp.zeros(
            (n_extra - BF16_ROWS_V7X, 2 * t), wq.dtype)

    def per_tile_pair(fn):
        def body(i, carry):
            for u in range(TILES_PER_TRIP):
                fn(TILES_PER_TRIP * i + u)
            return carry
        lax.fori_loop(0, nq // TILES_PER_TRIP, body, 0)

    def finalize(qtile, denominator=None):
        q0 = tile_start(qtile)
        if denominator is None:
            denominator = acc[qtile, V_HEAD_DIM:V_HEAD_DIM + 1]
        o2 = acc[qtile, :V_HEAD_DIM] * (1.0 / denominator)
        o = o2[:, :t] - lam * o2[:, t:]
        o = o * lax.rsqrt(jnp.mean(o * o, axis=0, keepdims=True) + EPS)
        o_ref[0, pl.ds(q0, t), :] = (o.T * gsub_ref[...]).astype(o_ref.dtype)

    @pl.when(fast_ref[layer] == 1)
    def _():
        _attn_fast_sweep(t, nq, slope2, bound_ref[layer], slope_pieces, pieces, stack_rows, build_wq,
                         finalize, kaug, vaug, wq, acc, m_all, dbias,
                         ((s2_even, p2_even), (s2_odd, p2_odd)))

    @pl.when(fast_ref[layer] == 0)
    def _():
        _attn_safe_sweep(t, nq, slope2, slope_pieces, stack_rows, build_wq, per_tile_pair, kaug,
                         vaug, wq, acc, m_all, s_even, s_odd, p_even, p_odd, a_even, a_odd, dbias)
        per_tile_pair(finalize)


def _attn_fast_sweep(t, nq, slope2, bound, slope_pieces, pieces, stack_rows, build_wq, finalize,
                     kaug, vaug, wq, acc, denom, dbias, bufs):
    tile_pieces = tuple(p * float(t) for p in slope_pieces)
    col = lax.broadcasted_iota(jnp.int32, (1, 2 * t), 1)
    col = jnp.where(col >= t, col - t, col)

    def position_rows(qtile):
        qpos = (qtile * t + col).astype(jnp.float32)
        return stack_rows(slope_pieces + tile_pieces + pieces(-(slope2 * qpos) - bound))

    steps = []
    for q in range(nq):
        steps += [(q, j, 2, False) for j in range(0, q - q % 2, 2)]
        steps.append((q, q, 1, True) if q % 2 == 0 else (q, q - 1, 2, True))
    first_step = {q: min(i for i, s in enumerate(steps) if s[0] == q) for q in range(nq)}
    last_step = {q: max(i for i, s in enumerate(steps) if s[0] == q) for q in range(nq)}

    def scores(i):
        q, j, ktiles, _ = steps[i]
        rows = ktiles * t
        bufs[i % 2][0][:rows] = jnp.dot(kaug[j * t:j * t + rows, :], wq[q],
                                        preferred_element_type=jnp.float32)

    def probabilities(i):
        q, _, ktiles, diag = steps[i]
        s_src, p_dst = bufs[i % 2]
        rows = ktiles * t
        below = rows - t if diag else rows
        total = None
        if below:
            p = jnp.exp2(s_src[:below])
            p_dst[:below] = p.astype(p_dst.dtype)
            total = jnp.sum(p, axis=0, keepdims=True)
        if diag:
            p = jnp.exp2(s_src[below:rows] + dbias[...])
            p_dst[below:rows] = p.astype(p_dst.dtype)
            part = jnp.sum(p, axis=0, keepdims=True)
            total = part if total is None else total + part
        denom[q] = total if i == first_step[q] else denom[q] + total

    def weighted_values(i):
        q, j, ktiles, _ = steps[i]
        rows = ktiles * t
        pv = jnp.dot(vaug[:V_HEAD_DIM, j * t:j * t + rows], bufs[i % 2][1][:rows],
                     preferred_element_type=jnp.float32)
        acc[q, :V_HEAD_DIM] = pv if i == first_step[q] else acc[q, :V_HEAD_DIM] + pv

    wq_rows = functools.partial(build_wq, position_rows=position_rows)
    wq_rows(0)
    wq_rows(1)
    scores(0)
    probabilities(0)
    scores(1)
    for i, (q, _, _, _) in enumerate(steps):
        if i == first_step[q] and q + 2 < nq:
            wq_rows(q + 2)
        if i + 2 < len(steps):
            scores(i + 2)
        if i + 1 < len(steps):
            probabilities(i + 1)
        weighted_values(i)
        if i == last_step[q]:
            finalize(q, denom[q])


def _attn_safe_sweep(t, nq, slope2, slope_pieces, stack_rows, build_wq, per_tile_pair, kaug, vaug,
                     wq, acc, m_all, s_even, s_odd, p_even, p_odd, a_even, a_odd, dbias):
    only_offset_rows = stack_rows(slope_pieces)
    per_tile_pair(functools.partial(build_wq, position_rows=lambda qtile: only_offset_rows))

    def scores(qtile, ktile, s_dst):
        k0 = pl.multiple_of(ktile * t, t)
        s_dst[...] = jnp.dot(kaug[pl.ds(k0, t), :], wq[qtile],
                             preferred_element_type=jnp.float32)

    def softmax(qtile, ktile, s_src, p_dst, a_dst, diag):
        s = s_src[...]
        if diag:
            s = s + dbias[...]
            m_new = jnp.max(s, axis=0, keepdims=True)
            p_dst[...] = jnp.exp2(s - m_new).astype(p_dst.dtype)
        else:
            shift = slope2 * ((ktile - qtile) * t).astype(jnp.float32)
            m_old = m_all[qtile]
            m_new = jnp.maximum(m_old, jnp.max(s, axis=0, keepdims=True) + shift)
            p_dst[...] = jnp.exp2(s - (m_new - shift)).astype(p_dst.dtype)
            a_dst[...] = jnp.exp2(m_old - m_new)
        m_all[qtile] = m_new

    def weighted_values(qtile, ktile, p_src, a_src, diag):
        k0 = pl.multiple_of(ktile * t, t)
        pv = jnp.dot(vaug[:, pl.ds(k0, t)], p_src[...], preferred_element_type=jnp.float32)
        acc[qtile] = pv if diag else a_src[...] * acc[qtile] + pv

    bufs = ((s_even, p_even, a_even), (s_odd, p_odd, a_odd))

    spare = (jnp.int32(nq), jnp.int32(0))
    p_odd[...] = jnp.zeros_like(p_odd)
    a_odd[...] = jnp.ones_like(a_odd)
    acc[nq] = jnp.zeros(acc.shape[1:], acc.dtype)

    def run_pairs(first, advance, n_steps, unroll, diag):
        assert n_steps % unroll == 0 and unroll % 2 == 0
        scores(*first, s_even)

        def body(_, state):
            for k in range(unroll):
                (s_cur, p_cur, a_cur), (s_nxt, p_prev, a_prev) = bufs[k % 2], bufs[(k + 1) % 2]
                q, j, qp, jp = state
                qn, jn = advance(q, j)
                qn, jn = jnp.minimum(qn, nq - 1), jnp.minimum(jn, nq - 1)
                scores(qn, jn, s_nxt)
                softmax(q, j, s_cur, p_cur, a_cur, diag)
                weighted_values(qp, jp, p_prev, a_prev, diag)
                state = (qn, jn, q, j)
            return state

        state = lax.fori_loop(0, n_steps // unroll, body, (*first, *spare))
        weighted_values(state[2], state[3], p_odd, a_odd, diag)

    i32 = jnp.int32
    run_pairs((i32(0), i32(0)), lambda q, j: (q + 1, j + 1), nq, DIAG_UNROLL, True)

    def below_diag(q, j):
        wrap = j + 1 == q
        return jnp.where(wrap, q + 1, q), jnp.where(wrap, 0, j + 1)

    run_pairs((i32(1), i32(0)), below_diag, nq * (nq - 1) // 2, PAIR_UNROLL, False)


def _attn_call(layer, slopes, lq, gq_col, gk_max, gsub_row, qt, kn, vt, lambda_init):
    batch, _, seq = qt.shape
    t = ATTN_TILE
    nq = seq // t
    assert nq % TILES_PER_TRIP == 0
    assert nq % DIAG_UNROLL == 0 and (nq * (nq - 1) // 2) % PAIR_UNROLL == 0
    assert t <= 256, "in-tile key offsets must be exact in bf16"
    kernel = functools.partial(_attn_kernel, seq=seq, lambda_init=lambda_init, layer=layer)
    f32, bf = jnp.float32, jnp.bfloat16
    bound = (BOUND_MARGIN * HEAD_DIM * jnp.max(jnp.abs(gq_col), axis=(1, 2)) * gk_max).astype(f32)
    fast = (bound <= FAST_BOUND_MAX).astype(jnp.int32)
    return pl.pallas_call(
        kernel,
        grid=(batch, N_HEADS),
        in_specs=[
            pl.BlockSpec(memory_space=pltpu.SMEM),
            pl.BlockSpec(memory_space=pltpu.SMEM),
            pl.BlockSpec(memory_space=pltpu.SMEM),
            _layer_spec(lq, layer),
            _layer_spec(gq_col, layer),
            _layer_spec(gsub_row, layer),
            pl.BlockSpec((1, 2 * HEAD_DIM, seq), lambda b, h: (b, h, 0)),
            pl.BlockSpec((1, seq, 2 * HEAD_DIM), lambda b, h: (b, 0, h)),
            pl.BlockSpec((1, V_HEAD_DIM, seq), lambda b, h: (b, h, 0)),
        ],
        out_specs=pl.BlockSpec((1, seq, V_HEAD_DIM), lambda b, h: (b, 0, h)),
        out_shape=jax.ShapeDtypeStruct((batch, seq, V_DIM), bf),
        scratch_shapes=[
            pltpu.VMEM((seq, K_AUG), bf),
            pltpu.VMEM((V_AUG, seq), bf),
            pltpu.VMEM((nq, K_AUG, 2 * t), bf),
            pltpu.VMEM((nq + 1, V_AUG, 2 * t), f32),
            pltpu.VMEM((nq, 1, 2 * t), f32),
            pltpu.VMEM((t, 2 * t), f32),
            pltpu.VMEM((t, 2 * t), f32),
            pltpu.VMEM((t, 2 * t), bf),
            pltpu.VMEM((t, 2 * t), bf),
            pltpu.VMEM((1, 2 * t), f32),
            pltpu.VMEM((1, 2 * t), f32),
            pltpu.VMEM((t, 2 * t), f32),
            pltpu.VMEM((2 * t, 2 * t), f32),
            pltpu.VMEM((2 * t, 2 * t), f32),
            pltpu.VMEM((2 * t, 2 * t), bf),
            pltpu.VMEM((2 * t, 2 * t), bf),
        ],
        compiler_params=pltpu.CompilerParams(
            dimension_semantics=("arbitrary", "arbitrary"), vmem_limit_bytes=VMEM_LIMIT_BYTES),
        name="attn",
    )(slopes, fast, bound, lq, gq_col, gsub_row, qt, kn, vt)


def _mix_kernel(x_ref, u_ref, halo_ref, ya_ref, gate_ref, wpg_ref, ps_ref, wbp_ref, wba_ref,
                wout_ref, gffn_ref, wup_ref, wdown_ref, o_ref, *, tiles_per_seq):
    tm = TOKEN_TILE
    i = pl.program_id(0)
    tile_in_seq = i % tiles_per_seq

    halo = jnp.where(tile_in_seq == 0, 0.0, halo_ref[...].astype(jnp.float32))
    u = u_ref[...].astype(jnp.float32)
    ext = jnp.concatenate([halo, u], axis=0)
    pos = tile_in_seq * tm + lax.broadcasted_iota(jnp.int32, (tm, POOL_GROUP_DIM), 0)
    win = ext
    ys = []
    for g, w in enumerate(POOL_WINDOWS):
        win = win[:, POOL_GROUP_DIM * (1 if g else 0):]
        win = win + pltpu.roll(win, w // 2, 0)
        cnt = jnp.minimum(pos + 1, w).astype(jnp.float32)
        ug = u[:, g * POOL_GROUP_DIM:(g + 1) * POOL_GROUP_DIM]
        mixed = win[POOL_HALO:, :POOL_GROUP_DIM] / cnt - ug
        ys.append(jnp.dot(mixed.astype(jnp.bfloat16), wpg_ref[g],
                          preferred_element_type=jnp.float32))
    y_pool = (jnp.concatenate(ys, axis=1) * ps_ref[...]).astype(jnp.bfloat16)

    bp = jnp.dot(y_pool, wbp_ref[...], preferred_element_type=jnp.float32)
    ba = jnp.dot(ya_ref[...], wba_ref[...], preferred_element_type=jnp.float32)
    gates = gate_ref[...].astype(jnp.float32)
    merged = gates[:, :D_MODEL] * bp + gates[:, D_MODEL:] * ba
    x1 = x_ref[...] + jnp.dot(merged.astype(jnp.bfloat16), wout_ref[...],
                              preferred_element_type=jnp.float32)

    ms = jnp.mean(x1 * x1, axis=-1, keepdims=True)
    h2 = (x1 * lax.rsqrt(ms + EPS) * gffn_ref[...]).astype(jnp.bfloat16)
    y = x1
    for cidx in range(D_FF // FF_CHUNK):
        sl = slice(cidx * FF_CHUNK, (cidx + 1) * FF_CHUNK)
        up = jnp.maximum(jnp.dot(h2, wup_ref[:, sl], preferred_element_type=jnp.float32), 0.0)
        y = y + jnp.dot((up * up).astype(jnp.bfloat16), wdown_ref[sl, :],
                        preferred_element_type=jnp.float32)
    o_ref[...] = y


def _mix_call(layer, x2d, u, y_attn, gates, w_pg, pool_scale, w_bp, w_ba, w_out, g_ffn, w_up,
              w_down, seq):
    tokens = x2d.shape[0]
    tm = TOKEN_TILE
    row = lambda i: (i, 0)
    halo_blocks = tm // POOL_HALO
    kernel = functools.partial(_mix_kernel, tiles_per_seq=seq // tm)
    return pl.pallas_call(
        kernel,
        grid=(tokens // tm,),
        in_specs=[
            pl.BlockSpec((tm, D_MODEL), row),
            pl.BlockSpec((tm, POOL_DIM), row),
            pl.BlockSpec((POOL_HALO, POOL_DIM),
                         lambda i: (jnp.maximum(i * halo_blocks - 1, 0), 0)),
            pl.BlockSpec((tm, V_DIM), row),
            pl.BlockSpec((tm, 2 * D_MODEL), row),
            _layer_spec(w_pg, layer),
            _layer_spec(pool_scale, layer),
            _layer_spec(w_bp, layer),
            _layer_spec(w_ba, layer),
            _layer_spec(w_out, layer),
            _layer_spec(g_ffn, layer),
            _layer_spec(w_up, layer),
            _layer_spec(w_down, layer),
        ],
        out_specs=pl.BlockSpec((tm, D_MODEL), row),
        out_shape=jax.ShapeDtypeStruct((tokens, D_MODEL), jnp.float32),
        compiler_params=pltpu.CompilerParams(
            dimension_semantics=("arbitrary",), vmem_limit_bytes=VMEM_LIMIT_BYTES),
        name="mix",
    )(x2d, u, u, y_attn, gates, w_pg, pool_scale, w_bp, w_ba, w_out, g_ffn, w_up, w_down)


def kernel(x, g_mix, w_in, w_pool_grp, pool_scale, g_q, g_k, lambda_qk, g_sub, w_branch_pool,
           w_branch_attn, w_gate, b_gate, w_out, g_ffn, w_up, w_down):
    batch, seq, d_model = x.shape
    depth = w_in.shape[0]
    assert d_model == D_MODEL and seq % TOKEN_TILE == 0 and seq % ATTN_TILE == 0
    assert TOKEN_TILE % POOL_HALO == 0 and POOL_HALO >= max(POOL_WINDOWS) - 1
    bf = jnp.bfloat16
    f32 = jnp.float32

    slopes = jnp.asarray([2.0 ** (-8.0 * (i + 1) / N_HEADS) for i in range(N_HEADS)], f32)
    grp = jnp.arange(MXU_COLS_V7X) // HEAD_DIM
    e_blk = (grp[:, None] == grp[None, :]).astype(bf)

    lambda_inits = [0.8 - 0.6 * math.exp(-0.3 * l) for l in range(depth)]
    k_lo, v_lo = POOL_DIM + QK_DIM, POOL_DIM + 2 * QK_DIM
    w_pk = jnp.concatenate([w_in[:, :, :POOL_DIM], w_in[:, :, k_lo:v_lo]], axis=2).astype(bf)
    w_qvt = jnp.concatenate([w_in[:, :, POOL_DIM:k_lo], w_in[:, :, v_lo:]], axis=2)
    w_qvt = jnp.swapaxes(w_qvt, 1, 2).astype(bf)
    row_vec = lambda a: a.astype(f32).reshape(depth, 1, -1)
    gk_row = row_vec(jnp.tile(g_k, (1, 2 * N_HEADS)))
    gk_max = jnp.max(jnp.abs(g_k.astype(f32)), axis=1)
    gq_col = (g_q.astype(f32) * (HEAD_DIM ** -0.5 * LOG2E)).reshape(depth, HEAD_DIM, 1)
    gsub_row = row_vec(g_sub) * (1.0 - jnp.asarray(lambda_inits, f32)).reshape(depth, 1, 1)
    g_mix_r, b_gate_r, pool_scale_r, g_ffn_r = map(row_vec, (g_mix, b_gate, pool_scale, g_ffn))
    lq = lambda_qk.astype(f32)
    w_gate_b, w_pg_b, w_bp_b, w_ba_b, w_out_b, w_up_b, w_down_b = (
        a.astype(bf) for a in (w_gate, w_pool_grp, w_branch_pool, w_branch_attn, w_out, w_up, w_down))

    x2d = x.reshape(batch * seq, d_model)
    for l in range(depth):
        u, kn, qt, vt, gates = _proj_call(l, x2d, g_mix_r, w_pk, w_qvt, w_gate_b, b_gate_r, gk_row,
                                          e_blk, batch, seq)
        y_attn = _attn_call(l, slopes, lq, gq_col, gk_max, gsub_row, qt,
                            kn.reshape(batch, seq, QK_DIM), vt, lambda_inits[l])
        x2d = _mix_call(l, x2d, u, y_attn.reshape(batch * seq, V_DIM), gates, w_pg_b, pool_scale_r,
                        w_bp_b, w_ba_b, w_out_b, g_ffn_r, w_up_b, w_down_b, seq)
    return x2d.reshape(batch, seq, d_model)
```

```python
import functools
import math

import jax
import jax.numpy as jnp
from jax import lax
from jax.experimental import pallas as pl
from jax.experimental.pallas import tpu as pltpu

D_MODEL = 1024
CHUNK = 64
POOL_WINDOWS = (2, 4, 8, 16)
N_POOL_GROUPS = len(POOL_WINDOWS)
POOL_GROUP_DIM = 128
POOL_DIM = N_POOL_GROUPS * POOL_GROUP_DIM
N_HEADS = 8
HEAD_DIM = 64
V_HEAD_DIM = 2 * HEAD_DIM
QK_DIM = N_HEADS * 2 * HEAD_DIM
V_DIM = N_HEADS * V_HEAD_DIM
D_FF = 4 * D_MODEL
EPS = 1e-6
LOG2E = math.log2(math.e)

MXU_COLS_V7X = 256
BF16_ROWS_V7X = 16
VMEM_LIMIT_BYTES = 56 * 1024 * 1024

TOKEN_TILE = 512
ATTN_TILE = 256
TILES_PER_TRIP = 4
DIAG_UNROLL = 4
PAIR_UNROLL = 8
FAST_BOUND_MAX = 48.0
BOUND_MARGIN = 1.02
POOL_HALO = 16
FF_CHUNK = 1024
MASK_VALUE = -1e30

K_AUG = 2 * HEAD_DIM + MXU_COLS_V7X // 2
N_PIECES = 3
V_AUG = V_HEAD_DIM + BF16_ROWS_V7X


def _const_spec(shape):
    nd = len(shape)
    return pl.BlockSpec(shape, lambda *_: (0,) * nd, pipeline_mode=pl.Buffered(1))


def _layer_spec(stacked, layer):
    rest = stacked.shape[1:]
    return pl.BlockSpec((None,) + rest, lambda *_: (layer,) + (0,) * len(rest),
                        pipeline_mode=pl.Buffered(1))


def _proj_kernel(x_ref, g_ref, wpk_ref, wqvt_ref, wg_ref, bg_ref, gk_ref, e_ref,
                 u_ref, kn_ref, qt_ref, vt_ref, gate_ref):
    xf = x_ref[...]
    ms = jnp.mean(xf * xf, axis=-1, keepdims=True)
    h = (xf * lax.rsqrt(ms + EPS) * g_ref[...]).astype(jnp.bfloat16)

    zpk = jnp.dot(h, wpk_ref[...], preferred_element_type=jnp.float32)
    u_ref[...] = zpk[:, :POOL_DIM].astype(u_ref.dtype)

    zk = zpk[:, POOL_DIM:]
    sq = (zk * zk).astype(jnp.bfloat16)
    e = e_ref[...]
    ssq = jnp.concatenate(
        [jnp.dot(sq[:, c * MXU_COLS_V7X:(c + 1) * MXU_COLS_V7X], e, preferred_element_type=jnp.float32)
         for c in range(QK_DIM // MXU_COLS_V7X)], axis=1)
    kn = zk * lax.rsqrt(ssq * (1.0 / HEAD_DIM) + EPS) * gk_ref[...]
    kn_ref[...] = kn.astype(kn_ref.dtype)

    qvt = lax.dot_general(wqvt_ref[...], h, (((1,), (1,)), ((), ())),
                          preferred_element_type=jnp.float32)
    qt_ref[0] = qvt[:QK_DIM].astype(qt_ref.dtype)
    vt_ref[0] = qvt[QK_DIM:].astype(vt_ref.dtype)

    zg = jnp.dot(h, wg_ref[...], preferred_element_type=jnp.float32) + bg_ref[...]
    gate_ref[...] = (1.0 / (1.0 + jnp.exp(-zg))).astype(gate_ref.dtype)


def _proj_call(layer, x2d, g_mix, w_pk, w_qvt, w_gate, b_gate, gk_row, e_blk, batch, seq):
    tokens = x2d.shape[0]
    tm = TOKEN_TILE
    nst = seq // tm
    bf = jnp.bfloat16
    row = lambda i: (i, 0)
    tr = lambda i: (i // nst, 0, i % nst)
    return pl.pallas_call(
        _proj_kernel,
        grid=(tokens // tm,),
        in_specs=[
            pl.BlockSpec((tm, D_MODEL), row),
            _layer_spec(g_mix, layer),
            _layer_spec(w_pk, layer),
            _layer_spec(w_qvt, layer),
            _layer_spec(w_gate, layer),
            _layer_spec(b_gate, layer),
            _layer_spec(gk_row, layer),
            _const_spec(e_blk.shape),
        ],
        out_specs=[
            pl.BlockSpec((tm, POOL_DIM), row),
            pl.BlockSpec((tm, QK_DIM), row),
            pl.BlockSpec((1, QK_DIM, tm), tr),
            pl.BlockSpec((1, V_DIM, tm), tr),
            pl.BlockSpec((tm, 2 * D_MODEL), row),
        ],
        out_shape=[
            jax.ShapeDtypeStruct((tokens, POOL_DIM), bf),
            jax.ShapeDtypeStruct((tokens, QK_DIM), bf),
            jax.ShapeDtypeStruct((batch, QK_DIM, seq), bf),
            jax.ShapeDtypeStruct((batch, V_DIM, seq), bf),
            jax.ShapeDtypeStruct((tokens, 2 * D_MODEL), bf),
        ],
        compiler_params=pltpu.CompilerParams(
            dimension_semantics=("arbitrary",), vmem_limit_bytes=VMEM_LIMIT_BYTES),
        name="proj",
    )(x2d, g_mix, w_pk, w_qvt, w_gate, b_gate, gk_row, e_blk)


def _attn_kernel(slopes_ref, fast_ref, bound_ref, lq_ref, gq_ref, gsub_ref, qt_ref, k_ref, vt_ref,
                 o_ref, kaug, vaug, wq, acc, m_all, s_even, s_odd, p_even, p_odd, a_even, a_odd,
                 dbias, s2_even, s2_odd, p2_even, p2_odd, *, seq, lambda_init, layer):
    t = ATTN_TILE
    nq = seq // t
    slope2 = slopes_ref[pl.program_id(1)] * LOG2E
    n_extra = K_AUG - 2 * HEAD_DIM

    def fill_operands():
        row = lax.broadcasted_iota(jnp.int32, (t, n_extra), 0).astype(jnp.float32)
        lane = lax.broadcasted_iota(jnp.int32, (t, n_extra), 1)
        for i in range(nq):
            kaug[i * t:(i + 1) * t, :2 * HEAD_DIM] = k_ref[0, i * t:(i + 1) * t, :]
            cols = jnp.where(lane < N_PIECES, row,
                             jnp.where(lane < 2 * N_PIECES, float(i),
                                       jnp.where(lane < 3 * N_PIECES, 1.0, 0.0)))
            kaug[i * t:(i + 1) * t, 2 * HEAD_DIM:] = cols.astype(kaug.dtype)
            vaug[:V_HEAD_DIM, i * t:(i + 1) * t] = vt_ref[0, :, i * t:(i + 1) * t]
        vaug[V_HEAD_DIM:, :] = jnp.ones((V_AUG - V_HEAD_DIM, seq), vaug.dtype)
        r = lax.broadcasted_iota(jnp.int32, (t, t), 0)
        c = lax.broadcasted_iota(jnp.int32, (t, t), 1)
        ahead = slope2 * (2.0 * (c - r).astype(jnp.float32))
        one_map = jnp.where((r // CHUNK) <= (c // CHUNK), jnp.where(r <= c, 0.0, ahead), MASK_VALUE)
        dbias[:, :t] = one_map
        dbias[:, t:] = one_map

    def pieces(x):
        hi = x.astype(jnp.bfloat16).astype(jnp.float32)
        mid = (x - hi).astype(jnp.bfloat16).astype(jnp.float32)
        return hi, mid, x - hi - mid

    def stack_rows(groups):
        prow = lax.broadcasted_iota(jnp.int32, (BF16_ROWS_V7X, 2 * t), 0)
        out = jnp.zeros((BF16_ROWS_V7X, 2 * t), jnp.float32)
        for idx, rowv in enumerate(groups):
            out = jnp.where(prow == idx, rowv, out)
        return out.astype(wq.dtype)

    slope_pieces = pieces(jnp.full((1, 2 * t), slope2, jnp.float32))

    lq = lq_ref[...]
    lam = (jnp.exp(jnp.sum(lq[0:1] * lq[1:2], axis=1, keepdims=True))
           - jnp.exp(jnp.sum(lq[2:3] * lq[3:4], axis=1, keepdims=True)) + lambda_init)
    gq = gq_ref[...]

    def tile_start(tile):
        return tile * t if isinstance(tile, int) else pl.multiple_of(tile * t, t)

    def build_wq(qtile, position_rows):
        q0 = tile_start(qtile)
        qt = qt_ref[0, :, pl.ds(q0, t)].astype(jnp.float32)
        zeros = jnp.zeros((HEAD_DIM, t), jnp.float32)
        for half in range(2):
            qh = qt[half * HEAD_DIM:(half + 1) * HEAD_DIM]
            qh = qh * lax.rsqrt(jnp.mean(qh * qh, axis=0, keepdims=True) + EPS) * gq
            both = jnp.concatenate([zeros, qh] if half else [qh, zeros], axis=1)
            wq[qtile, half * HEAD_DIM:(half + 1) * HEAD_DIM, :] = both.astype(wq.dtype)
        wq[qtile, 2 * HEAD_DIM:2 * HEAD_DIM + BF16_ROWS_V7X, :] = position_rows(qtile)
        wq[qtile, 2 * HEAD_DIM + BF16_ROWS_V7X:, :] = jnp.zeros(
            (n_extra - BF16_ROWS_V7X, 2 * t), wq.dtype)

    def per_tile_pair(fn):
        def body(i, carry):
            for u in range(TILES_PER_TRIP):
                fn(TILES_PER_TRIP * i + u)
            return carry
        lax.fori_loop(0, nq // TILES_PER_TRIP, body, 0)

    def finalize(qtile, denominator=None):
        q0 = tile_start(qtile)
        if denominator is None:
            denominator = acc[qtile, V_HEAD_DIM:V_HEAD_DIM + 1]
        o2 = acc[qtile, :V_HEAD_DIM] * (1.0 / denominator)
        o = o2[:, :t] - lam * o2[:, t:]
        o = o * lax.rsqrt(jnp.mean(o * o, axis=0, keepdims=True) + EPS)
        o_ref[0, pl.ds(q0, t), :] = (o.T * gsub_ref[...]).astype(o_ref.dtype)

    @pl.when(fast_ref[layer] == 1)
    def _():
        fill_operands()
        _attn_fast_sweep(t, nq, slope2, bound_ref[layer], slope_pieces, pieces, stack_rows, build_wq,
                         finalize, kaug, vaug, wq, acc, m_all, dbias,
                         ((s2_even, p2_even), (s2_odd, p2_odd)))

    @pl.when(fast_ref[layer] == 0)
    def _():
        fill_operands()
        _attn_safe_sweep(t, nq, slope2, slope_pieces, stack_rows, build_wq, per_tile_pair, kaug,
                         vaug, wq, acc, m_all, s_even, s_odd, p_even, p_odd, a_even, a_odd, dbias)
        per_tile_pair(finalize)


def _attn_fast_sweep(t, nq, slope2, bound, slope_pieces, pieces, stack_rows, build_wq, finalize,
                     kaug, vaug, wq, acc, denom, dbias, bufs):
    tile_pieces = tuple(p * float(t) for p in slope_pieces)
    col = lax.broadcasted_iota(jnp.int32, (1, 2 * t), 1)
    col = jnp.where(col >= t, col - t, col)

    def position_rows(qtile):
        qpos = (qtile * t + col).astype(jnp.float32)
        return stack_rows(slope_pieces + tile_pieces + pieces(-(slope2 * qpos) - bound))

    steps = []
    for q in range(nq):
        steps += [(q, j, 2, False) for j in range(0, q - q % 2, 2)]
        steps.append((q, q, 1, True) if q % 2 == 0 else (q, q - 1, 2, True))
    first_step = {q: min(i for i, s in enumerate(steps) if s[0] == q) for q in range(nq)}
    last_step = {q: max(i for i, s in enumerate(steps) if s[0] == q) for q in range(nq)}

    def scores(i):
        q, j, ktiles, _ = steps[i]
        rows = ktiles * t
        bufs[i % 2][0][:rows] = jnp.dot(kaug[j * t:j * t + rows, :], wq[q],
                                        preferred_element_type=jnp.float32)

    def probabilities(i):
        q, _, ktiles, diag = steps[i]
        s_src, p_dst = bufs[i % 2]
        rows = ktiles * t
        below = rows - t if diag else rows
        total = None
        if below:
            p = jnp.exp2(s_src[:below])
            p_dst[:below] = p.astype(p_dst.dtype)
            total = jnp.sum(p, axis=0, keepdims=True)
        if diag:
            p = jnp.exp2(s_src[below:rows] + dbias[...])
            p_dst[below:rows] = p.astype(p_dst.dtype)
            part = jnp.sum(p, axis=0, keepdims=True)
            total = part if total is None else total + part
        denom[q] = total if i == first_step[q] else denom[q] + total

    def weighted_values(i):
        q, j, ktiles, _ = steps[i]
        rows = ktiles * t
        pv = jnp.dot(vaug[:V_HEAD_DIM, j * t:j * t + rows], bufs[i % 2][1][:rows],
                     preferred_element_type=jnp.float32)
        acc[q, :V_HEAD_DIM] = pv if i == first_step[q] else acc[q, :V_HEAD_DIM] + pv

    wq_rows = functools.partial(build_wq, position_rows=position_rows)
    wq_rows(0)
    wq_rows(1)
    scores(0)
    probabilities(0)
    scores(1)
    for i, (q, _, _, _) in enumerate(steps):
        if i == first_step[q] and q + 2 < nq:
            wq_rows(q + 2)
        if i + 2 < len(steps):
            scores(i + 2)
        if i + 1 < len(steps):
            probabilities(i + 1)
        weighted_values(i)
        if i == last_step[q]:
            finalize(q, denom[q])


def _attn_safe_sweep(t, nq, slope2, slope_pieces, stack_rows, build_wq, per_tile_pair, kaug, vaug,
                     wq, acc, m_all, s_even, s_odd, p_even, p_odd, a_even, a_odd, dbias):
    only_offset_rows = stack_rows(slope_pieces)
    per_tile_pair(functools.partial(build_wq, position_rows=lambda qtile: only_offset_rows))

    def scores(qtile, ktile, s_dst):
        k0 = pl.multiple_of(ktile * t, t)
        s_dst[...] = jnp.dot(kaug[pl.ds(k0, t), :], wq[qtile],
                             preferred_element_type=jnp.float32)

    def softmax(qtile, ktile, s_src, p_dst, a_dst, diag):
        s = s_src[...]
        if diag:
            s = s + dbias[...]
            m_new = jnp.max(s, axis=0, keepdims=True)
            p_dst[...] = jnp.exp2(s - m_new).astype(p_dst.dtype)
        else:
            shift = slope2 * ((ktile - qtile) * t).astype(jnp.float32)
            m_old = m_all[qtile]
            m_new = jnp.maximum(m_old, jnp.max(s, axis=0, keepdims=True) + shift)
            p_dst[...] = jnp.exp2(s - (m_new - shift)).astype(p_dst.dtype)
            a_dst[...] = jnp.exp2(m_old - m_new)
        m_all[qtile] = m_new

    def weighted_values(qtile, ktile, p_src, a_src, diag):
        k0 = pl.multiple_of(ktile * t, t)
        pv = jnp.dot(vaug[:, pl.ds(k0, t)], p_src[...], preferred_element_type=jnp.float32)
        acc[qtile] = pv if diag else a_src[...] * acc[qtile] + pv

    bufs = ((s_even, p_even, a_even), (s_odd, p_odd, a_odd))

    spare = (jnp.int32(nq), jnp.int32(0))
    p_odd[...] = jnp.zeros_like(p_odd)
    a_odd[...] = jnp.ones_like(a_odd)
    acc[nq] = jnp.zeros(acc.shape[1:], acc.dtype)

    def run_pairs(first, advance, n_steps, unroll, diag):
        assert n_steps % unroll == 0 and unroll % 2 == 0
        scores(*first, s_even)

        def body(_, state):
            for k in range(unroll):
                (s_cur, p_cur, a_cur), (s_nxt, p_prev, a_prev) = bufs[k % 2], bufs[(k + 1) % 2]
                q, j, qp, jp = state
                qn, jn = advance(q, j)
                qn, jn = jnp.minimum(qn, nq - 1), jnp.minimum(jn, nq - 1)
                scores(qn, jn, s_nxt)
                softmax(q, j, s_cur, p_cur, a_cur, diag)
                weighted_values(qp, jp, p_prev, a_prev, diag)
                state = (qn, jn, q, j)
            return state

        state = lax.fori_loop(0, n_steps // unroll, body, (*first, *spare))
        weighted_values(state[2], state[3], p_odd, a_odd, diag)

    i32 = jnp.int32
    run_pairs((i32(0), i32(0)), lambda q, j: (q + 1, j + 1), nq, DIAG_UNROLL, True)

    def below_diag(q, j):
        wrap = j + 1 == q
        return jnp.where(wrap, q + 1, q), jnp.where(wrap, 0, j + 1)

    run_pairs((i32(1), i32(0)), below_diag, nq * (nq - 1) // 2, PAIR_UNROLL, False)


def _attn_call(layer, slopes, lq, gq_col, gk_max, gsub_row, qt, kn, vt, lambda_init):
    batch, _, seq = qt.shape
    t = ATTN_TILE
    nq = seq // t
    assert nq % TILES_PER_TRIP == 0
    assert nq % DIAG_UNROLL == 0 and (nq * (nq - 1) // 2) % PAIR_UNROLL == 0
    assert t <= 256, "in-tile key offsets must be exact in bf16"
    kernel = functools.partial(_attn_kernel, seq=seq, lambda_init=lambda_init, layer=layer)
    f32, bf = jnp.float32, jnp.bfloat16
    bound = (BOUND_MARGIN * HEAD_DIM * jnp.max(jnp.abs(gq_col), axis=(1, 2)) * gk_max).astype(f32)
    fast = (bound <= FAST_BOUND_MAX).astype(jnp.int32)
    return pl.pallas_call(
        kernel,
        grid=(batch, N_HEADS),
        in_specs=[
            pl.BlockSpec(memory_space=pltpu.SMEM),
            pl.BlockSpec(memory_space=pltpu.SMEM),
            pl.BlockSpec(memory_space=pltpu.SMEM),
            _layer_spec(lq, layer),
            _layer_spec(gq_col, layer),
            _layer_spec(gsub_row, layer),
            pl.BlockSpec((1, 2 * HEAD_DIM, seq), lambda b, h: (b, h, 0)),
            pl.BlockSpec((1, seq, 2 * HEAD_DIM), lambda b, h: (b, 0, h)),
            pl.BlockSpec((1, V_HEAD_DIM, seq), lambda b, h: (b, h, 0)),
        ],
        out_specs=pl.BlockSpec((1, seq, V_HEAD_DIM), lambda b, h: (b, 0, h)),
        out_shape=jax.ShapeDtypeStruct((batch, seq, V_DIM), bf),
        scratch_shapes=[
            pltpu.VMEM((seq, K_AUG), bf),
            pltpu.VMEM((V_AUG, seq), bf),
            pltpu.VMEM((nq, K_AUG, 2 * t), bf),
            pltpu.VMEM((nq + 1, V_AUG, 2 * t), f32),
            pltpu.VMEM((nq, 1, 2 * t), f32),
            pltpu.VMEM((t, 2 * t), f32),
            pltpu.VMEM((t, 2 * t), f32),
            pltpu.VMEM((t, 2 * t), bf),
            pltpu.VMEM((t, 2 * t), bf),
            pltpu.VMEM((1, 2 * t), f32),
            pltpu.VMEM((1, 2 * t), f32),
            pltpu.VMEM((t, 2 * t), f32),
            pltpu.VMEM((2 * t, 2 * t), f32),
            pltpu.VMEM((2 * t, 2 * t), f32),
            pltpu.VMEM((2 * t, 2 * t), bf),
            pltpu.VMEM((2 * t, 2 * t), bf),
        ],
        compiler_params=pltpu.CompilerParams(
            dimension_semantics=("arbitrary", "arbitrary"), vmem_limit_bytes=VMEM_LIMIT_BYTES),
        name="attn",
    )(slopes, fast, bound, lq, gq_col, gsub_row, qt, kn, vt)


def _mix_kernel(x_ref, u_ref, halo_ref, ya_ref, gate_ref, wpg_ref, ps_ref, wbp_ref, wba_ref,
                wout_ref, gffn_ref, wup_ref, wdown_ref, o_ref, *, tiles_per_seq):
    tm = TOKEN_TILE
    i = pl.program_id(0)
    tile_in_seq = i % tiles_per_seq

    halo = jnp.where(tile_in_seq == 0, 0.0, halo_ref[...].astype(jnp.float32))
    u = u_ref[...].astype(jnp.float32)
    ext = jnp.concatenate([halo, u], axis=0)
    pos = tile_in_seq * tm + lax.broadcasted_iota(jnp.int32, (tm, POOL_GROUP_DIM), 0)
    win = ext
    ys = []
    for g, w in enumerate(POOL_WINDOWS):
        win = win[:, POOL_GROUP_DIM * (1 if g else 0):]
        win = win + pltpu.roll(win, w // 2, 0)
        cnt = jnp.minimum(pos + 1, w).astype(jnp.float32)
        ug = u[:, g * POOL_GROUP_DIM:(g + 1) * POOL_GROUP_DIM]
        mixed = win[POOL_HALO:, :POOL_GROUP_DIM] / cnt - ug
        ys.append(jnp.dot(mixed.astype(jnp.bfloat16), wpg_ref[g],
                          preferred_element_type=jnp.float32))
    y_pool = (jnp.concatenate(ys, axis=1) * ps_ref[...]).astype(jnp.bfloat16)

    bp = jnp.dot(y_pool, wbp_ref[...], preferred_element_type=jnp.float32)
    ba = jnp.dot(ya_ref[...], wba_ref[...], preferred_element_type=jnp.float32)
    gates = gate_ref[...].astype(jnp.float32)
    merged = gates[:, :D_MODEL] * bp + gates[:, D_MODEL:] * ba
    x1 = x_ref[...] + jnp.dot(merged.astype(jnp.bfloat16), wout_ref[...],
                              preferred_element_type=jnp.float32)

    ms = jnp.mean(x1 * x1, axis=-1, keepdims=True)
    h2 = (x1 * lax.rsqrt(ms + EPS) * gffn_ref[...]).astype(jnp.bfloat16)
    y = x1
    for cidx in range(D_FF // FF_CHUNK):
        sl = slice(cidx * FF_CHUNK, (cidx + 1) * FF_CHUNK)
        up = jnp.maximum(jnp.dot(h2, wup_ref[:, sl], preferred_element_type=jnp.float32), 0.0)
        y = y + jnp.dot((up * up).astype(jnp.bfloat16), wdown_ref[sl, :],
                        preferred_element_type=jnp.float32)
    o_ref[...] = y


def _mix_call(layer, x2d, u, y_attn, gates, w_pg, pool_scale, w_bp, w_ba, w_out, g_ffn, w_up,
              w_down, seq):
    tokens = x2d.shape[0]
    tm = TOKEN_TILE
    row = lambda i: (i, 0)
    halo_blocks = tm // POOL_HALO
    kernel = functools.partial(_mix_kernel, tiles_per_seq=seq // tm)
    return pl.pallas_call(
        kernel,
        grid=(tokens // tm,),
        in_specs=[
            pl.BlockSpec((tm, D_MODEL), row),
            pl.BlockSpec((tm, POOL_DIM), row),
            pl.BlockSpec((POOL_HALO, POOL_DIM),
                         lambda i: (jnp.maximum(i * halo_blocks - 1, 0), 0)),
            pl.BlockSpec((tm, V_DIM), row),
            pl.BlockSpec((tm, 2 * D_MODEL), row),
            _layer_spec(w_pg, layer),
            _layer_spec(pool_scale, layer),
            _layer_spec(w_bp, layer),
            _layer_spec(w_ba, layer),
            _layer_spec(w_out, layer),
            _layer_spec(g_ffn, layer),
            _layer_spec(w_up, layer),
            _layer_spec(w_down, layer),
        ],
        out_specs=pl.BlockSpec((tm, D_MODEL), row),
        out_shape=jax.ShapeDtypeStruct((tokens, D_MODEL), jnp.float32),
        compiler_params=pltpu.CompilerParams(
            dimension_semantics=("arbitrary",), vmem_limit_bytes=VMEM_LIMIT_BYTES),
        name="mix",
    )(x2d, u, u, y_attn, gates, w_pg, pool_scale, w_bp, w_ba, w_out, g_ffn, w_up, w_down)


def kernel(x, g_mix, w_in, w_pool_grp, pool_scale, g_q, g_k, lambda_qk, g_sub, w_branch_pool,
           w_branch_attn, w_gate, b_gate, w_out, g_ffn, w_up, w_down):
    batch, seq, d_model = x.shape
    depth = w_in.shape[0]
    assert d_model == D_MODEL and seq % TOKEN_TILE == 0 and seq % ATTN_TILE == 0
    assert TOKEN_TILE % POOL_HALO == 0 and POOL_HALO >= max(POOL_WINDOWS) - 1
    bf = jnp.bfloat16
    f32 = jnp.float32

    slopes = jnp.asarray([2.0 ** (-8.0 * (i + 1) / N_HEADS) for i in range(N_HEADS)], f32)
    grp = jnp.arange(MXU_COLS_V7X) // HEAD_DIM
    e_blk = (grp[:, None] == grp[None, :]).astype(bf)

    lambda_inits = [0.8 - 0.6 * math.exp(-0.3 * l) for l in range(depth)]
    k_lo, v_lo = POOL_DIM + QK_DIM, POOL_DIM + 2 * QK_DIM
    w_pk = jnp.concatenate([w_in[:, :, :POOL_DIM], w_in[:, :, k_lo:v_lo]], axis=2).astype(bf)
    w_qvt = jnp.concatenate([w_in[:, :, POOL_DIM:k_lo], w_in[:, :, v_lo:]], axis=2)
    w_qvt = jnp.swapaxes(w_qvt, 1, 2).astype(bf)
    row_vec = lambda a: a.astype(f32).reshape(depth, 1, -1)
    gk_row = row_vec(jnp.tile(g_k, (1, 2 * N_HEADS)))
    gk_max = jnp.max(jnp.abs(g_k.astype(f32)), axis=1)
    gq_col = (g_q.astype(f32) * (HEAD_DIM ** -0.5 * LOG2E)).reshape(depth, HEAD_DIM, 1)
    gsub_row = row_vec(g_sub) * (1.0 - jnp.asarray(lambda_inits, f32)).reshape(depth, 1, 1)
    g_mix_r, b_gate_r, pool_scale_r, g_ffn_r = map(row_vec, (g_mix, b_gate, pool_scale, g_ffn))
    lq = lambda_qk.astype(f32)
    w_gate_b, w_pg_b, w_bp_b, w_ba_b, w_out_b, w_up_b, w_down_b = (
        a.astype(bf) for a in (w_gate, w_pool_grp, w_branch_pool, w_branch_attn, w_out, w_up, w_down))

    x2d = x.reshape(batch * seq, d_model)
    for l in range(depth):
        u, kn, qt, vt, gates = _proj_call(l, x2d, g_mix_r, w_pk, w_qvt, w_gate_b, b_gate_r, gk_row,
                                          e_blk, batch, seq)
        y_attn = _attn_call(l, slopes, lq, gq_col, gk_max, gsub_row, qt,
                            kn.reshape(batch, seq, QK_DIM), vt, lambda_inits[l])
        x2d = _mix_call(l, x2d, u, y_attn.reshape(batch * seq, V_DIM), gates, w_pg_b, pool_scale_r,
                        w_bp_b, w_ba_b, w_out_b, g_ffn_r, w_up_b, w_down_b, seq)
    return x2d.reshape(batch, seq, d_model)
```

```python
import functools
import math

import jax
import jax.numpy as jnp
from jax import lax
from jax.experimental import pallas as pl
from jax.experimental.pallas import tpu as pltpu

D_MODEL = 1024
CHUNK = 64
POOL_WINDOWS = (2, 4, 8, 16)
N_POOL_GROUPS = len(POOL_WINDOWS)
POOL_GROUP_DIM = 128
POOL_DIM = N_POOL_GROUPS * POOL_GROUP_DIM
N_HEADS = 8
HEAD_DIM = 64
V_HEAD_DIM = 2 * HEAD_DIM
QK_DIM = N_HEADS * 2 * HEAD_DIM
V_DIM = N_HEADS * V_HEAD_DIM
D_FF = 4 * D_MODEL
EPS = 1e-6
LOG2E = math.log2(math.e)

MXU_COLS_V7X = 256
BF16_ROWS_V7X = 16
VMEM_LIMIT_BYTES = 56 * 1024 * 1024

TOKEN_TILE = 512
ATTN_TILE = 256
TILES_PER_TRIP = 4
DIAG_UNROLL = 4
PAIR_UNROLL = 8
FAST_BOUND_MAX = 48.0
BOUND_MARGIN = 1.02
POOL_HALO = 16
FF_CHUNK = 1024
MASK_VALUE = -1e30

K_AUG = 2 * HEAD_DIM + MXU_COLS_V7X // 2
N_PIECES = 3
V_AUG = V_HEAD_DIM + BF16_ROWS_V7X


def _const_spec(shape):
    nd = len(shape)
    return pl.BlockSpec(shape, lambda *_: (0,) * nd, pipeline_mode=pl.Buffered(1))


def _layer_spec(stacked, layer):
    rest = stacked.shape[1:]
    return pl.BlockSpec((None,) + rest, lambda *_: (layer,) + (0,) * len(rest),
                        pipeline_mode=pl.Buffered(1))


def _proj_kernel(x_ref, g_ref, wpk_ref, wqvt_ref, wg_ref, bg_ref, gk_ref, e_ref,
                 u_ref, kn_ref, qt_ref, vt_ref, gate_ref):
    for part in range(2):
        rows = slice(part * (TOKEN_TILE // 2), (part + 1) * (TOKEN_TILE // 2))
        _proj_rows(x_ref, g_ref, wpk_ref, wqvt_ref, wg_ref, bg_ref, gk_ref, e_ref,
                   u_ref, kn_ref, qt_ref, vt_ref, gate_ref, rows)


def _proj_rows(x_ref, g_ref, wpk_ref, wqvt_ref, wg_ref, bg_ref, gk_ref, e_ref,
               u_ref, kn_ref, qt_ref, vt_ref, gate_ref, rows):
    xf = x_ref[rows, :]
    ms = jnp.mean(xf * xf, axis=-1, keepdims=True)
    h = (xf * lax.rsqrt(ms + EPS) * g_ref[...]).astype(jnp.bfloat16)

    zpk = jnp.dot(h, wpk_ref[...], preferred_element_type=jnp.float32)
    u_ref[rows, :] = zpk[:, :POOL_DIM].astype(u_ref.dtype)

    zk = zpk[:, POOL_DIM:]
    sq = (zk * zk).astype(jnp.bfloat16)
    e = e_ref[...]
    ssq = jnp.concatenate(
        [jnp.dot(sq[:, c * MXU_COLS_V7X:(c + 1) * MXU_COLS_V7X], e, preferred_element_type=jnp.float32)
         for c in range(QK_DIM // MXU_COLS_V7X)], axis=1)
    kn = zk * lax.rsqrt(ssq * (1.0 / HEAD_DIM) + EPS) * gk_ref[...]
    kn_ref[rows, :] = kn.astype(kn_ref.dtype)

    qvt = lax.dot_general(wqvt_ref[...], h, (((1,), (1,)), ((), ())),
                          preferred_element_type=jnp.float32)
    qt_ref[0, :, rows] = qvt[:QK_DIM].astype(qt_ref.dtype)
    vt_ref[0, :, rows] = qvt[QK_DIM:].astype(vt_ref.dtype)

    zg = jnp.dot(h, wg_ref[...], preferred_element_type=jnp.float32) + bg_ref[...]
    gate_ref[rows, :] = (1.0 / (1.0 + jnp.exp(-zg))).astype(gate_ref.dtype)


def _proj_call(layer, x2d, g_mix, w_pk, w_qvt, w_gate, b_gate, gk_row, e_blk, batch, seq):
    tokens = x2d.shape[0]
    tm = TOKEN_TILE
    nst = seq // tm
    bf = jnp.bfloat16
    row = lambda i: (i, 0)
    tr = lambda i: (i // nst, 0, i % nst)
    return pl.pallas_call(
        _proj_kernel,
        grid=(tokens // tm,),
        in_specs=[
            pl.BlockSpec((tm, D_MODEL), row),
            _layer_spec(g_mix, layer),
            _layer_spec(w_pk, layer),
            _layer_spec(w_qvt, layer),
            _layer_spec(w_gate, layer),
            _layer_spec(b_gate, layer),
            _layer_spec(gk_row, layer),
            _const_spec(e_blk.shape),
        ],
        out_specs=[
            pl.BlockSpec((tm, POOL_DIM), row),
            pl.BlockSpec((tm, QK_DIM), row),
            pl.BlockSpec((1, QK_DIM, tm), tr),
            pl.BlockSpec((1, V_DIM, tm), tr),
            pl.BlockSpec((tm, 2 * D_MODEL), row),
        ],
        out_shape=[
            jax.ShapeDtypeStruct((tokens, POOL_DIM), bf),
            jax.ShapeDtypeStruct((tokens, QK_DIM), bf),
            jax.ShapeDtypeStruct((batch, QK_DIM, seq), bf),
            jax.ShapeDtypeStruct((batch, V_DIM, seq), bf),
            jax.ShapeDtypeStruct((tokens, 2 * D_MODEL), bf),
        ],
        compiler_params=pltpu.CompilerParams(
            dimension_semantics=("arbitrary",), vmem_limit_bytes=VMEM_LIMIT_BYTES),
        name="proj",
    )(x2d, g_mix, w_pk, w_qvt, w_gate, b_gate, gk_row, e_blk)


def _attn_kernel(slopes_ref, fast_ref, bound_ref, lq_ref, gq_ref, gsub_ref, qt_ref, k_ref, vt_ref,
                 o_ref, kaug, vaug, wq, acc, m_all, s_even, s_odd, p_even, p_odd, a_even, a_odd,
                 dbias, s2_even, s2_odd, p2_even, p2_odd, *, seq, lambda_init, layer):
    t = ATTN_TILE
    nq = seq // t
    slope2 = slopes_ref[pl.program_id(1)] * LOG2E
    n_extra = K_AUG - 2 * HEAD_DIM

    def fill_operands():
        row = lax.broadcasted_iota(jnp.int32, (t, n_extra), 0).astype(jnp.float32)
        lane = lax.broadcasted_iota(jnp.int32, (t, n_extra), 1)
        for i in range(nq):
            kaug[i * t:(i + 1) * t, :2 * HEAD_DIM] = k_ref[0, i * t:(i + 1) * t, :]
            cols = jnp.where(lane < N_PIECES, row,
                             jnp.where(lane < 2 * N_PIECES, float(i),
                                       jnp.where(lane < 3 * N_PIECES, 1.0, 0.0)))
            kaug[i * t:(i + 1) * t, 2 * HEAD_DIM:] = cols.astype(kaug.dtype)
            vaug[:V_HEAD_DIM, i * t:(i + 1) * t] = vt_ref[0, :, i * t:(i + 1) * t]
        vaug[V_HEAD_DIM:, :] = jnp.ones((V_AUG - V_HEAD_DIM, seq), vaug.dtype)
        r = lax.broadcasted_iota(jnp.int32, (t, t), 0)
        c = lax.broadcasted_iota(jnp.int32, (t, t), 1)
        ahead = slope2 * (2.0 * (c - r).astype(jnp.float32))
        one_map = jnp.where((r // CHUNK) <= (c // CHUNK), jnp.where(r <= c, 0.0, ahead), MASK_VALUE)
        dbias[:, :t] = one_map
        dbias[:, t:] = one_map

    def pieces(x):
        hi = x.astype(jnp.bfloat16).astype(jnp.float32)
        mid = (x - hi).astype(jnp.bfloat16).astype(jnp.float32)
        return hi, mid, x - hi - mid

    def stack_rows(groups):
        prow = lax.broadcasted_iota(jnp.int32, (BF16_ROWS_V7X, 2 * t), 0)
        out = jnp.zeros((BF16_ROWS_V7X, 2 * t), jnp.float32)
        for idx, rowv in enumerate(groups):
            out = jnp.where(prow == idx, rowv, out)
        return out.astype(wq.dtype)

    slope_pieces = pieces(jnp.full((1, 2 * t), slope2, jnp.float32))

    lq = lq_ref[...]
    lam = (jnp.exp(jnp.sum(lq[0:1] * lq[1:2], axis=1, keepdims=True))
           - jnp.exp(jnp.sum(lq[2:3] * lq[3:4], axis=1, keepdims=True)) + lambda_init)
    gq = gq_ref[...]

    def tile_start(tile):
        return tile * t if isinstance(tile, int) else pl.multiple_of(tile * t, t)

    def build_wq(qtile, position_rows):
        q0 = tile_start(qtile)
        qt = qt_ref[0, :, pl.ds(q0, t)].astype(jnp.float32)
        zeros = jnp.zeros((HEAD_DIM, t), jnp.float32)
        for half in range(2):
            qh = qt[half * HEAD_DIM:(half + 1) * HEAD_DIM]
            qh = qh * lax.rsqrt(jnp.mean(qh * qh, axis=0, keepdims=True) + EPS) * gq
            both = jnp.concatenate([zeros, qh] if half else [qh, zeros], axis=1)
            wq[qtile, half * HEAD_DIM:(half + 1) * HEAD_DIM, :] = both.astype(wq.dtype)
        wq[qtile, 2 * HEAD_DIM:2 * HEAD_DIM + BF16_ROWS_V7X, :] = position_rows(qtile)
        wq[qtile, 2 * HEAD_DIM + BF16_ROWS_V7X:, :] = jnp.zeros(
            (n_extra - BF16_ROWS_V7X, 2 * t), wq.dtype)

    def per_tile_pair(fn):
        def body(i, carry):
            for u in range(TILES_PER_TRIP):
                fn(TILES_PER_TRIP * i + u)
            return carry
        lax.fori_loop(0, nq // TILES_PER_TRIP, body, 0)

    def finalize(qtile, denominator=None):
        q0 = tile_start(qtile)
        if denominator is None:
            denominator = acc[qtile, V_HEAD_DIM:V_HEAD_DIM + 1]
        o2 = acc[qtile, :V_HEAD_DIM] * (1.0 / denominator)
        o = o2[:, :t] - lam * o2[:, t:]
        o = o * lax.rsqrt(jnp.mean(o * o, axis=0, keepdims=True) + EPS)
        o_ref[0, pl.ds(q0, t), :] = (o.T * gsub_ref[...]).astype(o_ref.dtype)

    @pl.when(fast_ref[layer] == 1)
    def _():
        fill_operands()
        _attn_fast_sweep(t, nq, slope2, bound_ref[layer], slope_pieces, pieces, stack_rows, build_wq,
                         finalize, kaug, vaug, wq, acc, m_all, dbias,
                         ((s2_even, p2_even), (s2_odd, p2_odd)))

    @pl.when(fast_ref[layer] == 0)
    def _():
        fill_operands()
        _attn_safe_sweep(t, nq, slope2, slope_pieces, stack_rows, build_wq, per_tile_pair, kaug,
                         vaug, wq, acc, m_all, s_even, s_odd, p_even, p_odd, a_even, a_odd, dbias)
        per_tile_pair(finalize)


def _attn_fast_sweep(t, nq, slope2, bound, slope_pieces, pieces, stack_rows, build_wq, finalize,
                     kaug, vaug, wq, acc, denom, dbias, bufs):
    tile_pieces = tuple(p * float(t) for p in slope_pieces)
    col = lax.broadcasted_iota(jnp.int32, (1, 2 * t), 1)
    col = jnp.where(col >= t, col - t, col)

    def position_rows(qtile):
        qpos = (qtile * t + col).astype(jnp.float32)
        return stack_rows(slope_pieces + tile_pieces + pieces(-(slope2 * qpos) - bound))

    steps = []
    for q in range(nq):
        steps += [(q, j, 2, False) for j in range(0, q - q % 2, 2)]
        steps.append((q, q, 1, True) if q % 2 == 0 else (q, q - 1, 2, True))
    first_step = {q: min(i for i, s in enumerate(steps) if s[0] == q) for q in range(nq)}
    last_step = {q: max(i for i, s in enumerate(steps) if s[0] == q) for q in range(nq)}

    def scores(i):
        q, j, ktiles, _ = steps[i]
        rows = ktiles * t
        bufs[i % 2][0][:rows] = jnp.dot(kaug[j * t:j * t + rows, :], wq[q],
                                        preferred_element_type=jnp.float32)

    def probabilities(i):
        q, _, ktiles, diag = steps[i]
        s_src, p_dst = bufs[i % 2]
        rows = ktiles * t
        below = rows - t if diag else rows
        total = None
        if below:
            p = jnp.exp2(s_src[:below])
            p_dst[:below] = p.astype(p_dst.dtype)
            total = jnp.sum(p, axis=0, keepdims=True)
        if diag:
            p = jnp.exp2(s_src[below:rows] + dbias[...])
            p_dst[below:rows] = p.astype(p_dst.dtype)
            part = jnp.sum(p, axis=0, keepdims=True)
            total = part if total is None else total + part
        denom[q] = total if i == first_step[q] else denom[q] + total

    def weighted_values(i):
        q, j, ktiles, _ = steps[i]
        rows = ktiles * t
        pv = jnp.dot(vaug[:V_HEAD_DIM, j * t:j * t + rows], bufs[i % 2][1][:rows],
                     preferred_element_type=jnp.float32)
        acc[q, :V_HEAD_DIM] = pv if i == first_step[q] else acc[q, :V_HEAD_DIM] + pv

    wq_rows = functools.partial(build_wq, position_rows=position_rows)
    wq_rows(0)
    wq_rows(1)
    scores(0)
    probabilities(0)
    scores(1)
    for i, (q, _, _, _) in enumerate(steps):
        if i == first_step[q] and q + 2 < nq:
            wq_rows(q + 2)
        if i + 2 < len(steps):
            scores(i + 2)
        if i + 1 < len(steps):
            probabilities(i + 1)
        weighted_values(i)
        if i == last_step[q]:
            finalize(q, denom[q])


def _attn_safe_sweep(t, nq, slope2, slope_pieces, stack_rows, build_wq, per_tile_pair, kaug, vaug,
                     wq, acc, m_all, s_even, s_odd, p_even, p_odd, a_even, a_odd, dbias):
    only_offset_rows = stack_rows(slope_pieces)
    per_tile_pair(functools.partial(build_wq, position_rows=lambda qtile: only_offset_rows))

    def scores(qtile, ktile, s_dst):
        k0 = pl.multiple_of(ktile * t, t)
        s_dst[...] = jnp.dot(kaug[pl.ds(k0, t), :], wq[qtile],
                             preferred_element_type=jnp.float32)

    def softmax(qtile, ktile, s_src, p_dst, a_dst, diag):
        s = s_src[...]
        if diag:
            s = s + dbias[...]
            m_new = jnp.max(s, axis=0, keepdims=True)
            p_dst[...] = jnp.exp2(s - m_new).astype(p_dst.dtype)
        else:
            shift = slope2 * ((ktile - qtile) * t).astype(jnp.float32)
            m_old = m_all[qtile]
            m_new = jnp.maximum(m_old, jnp.max(s, axis=0, keepdims=True) + shift)
            p_dst[...] = jnp.exp2(s - (m_new - shift)).astype(p_dst.dtype)
            a_dst[...] = jnp.exp2(m_old - m_new)
        m_all[qtile] = m_new

    def weighted_values(qtile, ktile, p_src, a_src, diag):
        k0 = pl.multiple_of(ktile * t, t)
        pv = jnp.dot(vaug[:, pl.ds(k0, t)], p_src[...], preferred_element_type=jnp.float32)
        acc[qtile] = pv if diag else a_src[...] * acc[qtile] + pv

    bufs = ((s_even, p_even, a_even), (s_odd, p_odd, a_odd))

    spare = (jnp.int32(nq), jnp.int32(0))
    p_odd[...] = jnp.zeros_like(p_odd)
    a_odd[...] = jnp.ones_like(a_odd)
    acc[nq] = jnp.zeros(acc.shape[1:], acc.dtype)

    def run_pairs(first, advance, n_steps, unroll, diag):
        assert n_steps % unroll == 0 and unroll % 2 == 0
        scores(*first, s_even)

        def body(_, state):
            for k in range(unroll):
                (s_cur, p_cur, a_cur), (s_nxt, p_prev, a_prev) = bufs[k % 2], bufs[(k + 1) % 2]
                q, j, qp, jp = state
                qn, jn = advance(q, j)
                qn, jn = jnp.minimum(qn, nq - 1), jnp.minimum(jn, nq - 1)
                scores(qn, jn, s_nxt)
                softmax(q, j, s_cur, p_cur, a_cur, diag)
                weighted_values(qp, jp, p_prev, a_prev, diag)
                state = (qn, jn, q, j)
            return state

        state = lax.fori_loop(0, n_steps // unroll, body, (*first, *spare))
        weighted_values(state[2], state[3], p_odd, a_odd, diag)

    i32 = jnp.int32
    run_pairs((i32(0), i32(0)), lambda q, j: (q + 1, j + 1), nq, DIAG_UNROLL, True)

    def below_diag(q, j):
        wrap = j + 1 == q
        return jnp.where(wrap, q + 1, q), jnp.where(wrap, 0, j + 1)

    run_pairs((i32(1), i32(0)), below_diag, nq * (nq - 1) // 2, PAIR_UNROLL, False)


def _attn_call(layer, slopes, lq, gq_col, gk_max, gsub_row, qt, kn, vt, lambda_init):
    batch, _, seq = qt.shape
    t = ATTN_TILE
    nq = seq // t
    assert nq % TILES_PER_TRIP == 0
    assert nq % DIAG_UNROLL == 0 and (nq * (nq - 1) // 2) % PAIR_UNROLL == 0
    assert t <= 256, "in-tile key offsets must be exact in bf16"
    kernel = functools.partial(_attn_kernel, seq=seq, lambda_init=lambda_init, layer=layer)
    f32, bf = jnp.float32, jnp.bfloat16
    bound = (BOUND_MARGIN * HEAD_DIM * jnp.max(jnp.abs(gq_col), axis=(1, 2)) * gk_max).astype(f32)
    fast = (bound <= FAST_BOUND_MAX).astype(jnp.int32)
    return pl.pallas_call(
        kernel,
        grid=(batch, N_HEADS),
        in_specs=[
            pl.BlockSpec(memory_space=pltpu.SMEM),
            pl.BlockSpec(memory_space=pltpu.SMEM),
            pl.BlockSpec(memory_space=pltpu.SMEM),
            _layer_spec(lq, layer),
            _layer_spec(gq_col, layer),
            _layer_spec(gsub_row, layer),
            pl.BlockSpec((1, 2 * HEAD_DIM, seq), lambda b, h: (b, h, 0)),
            pl.BlockSpec((1, seq, 2 * HEAD_DIM), lambda b, h: (b, 0, h)),
            pl.BlockSpec((1, V_HEAD_DIM, seq), lambda b, h: (b, h, 0)),
        ],
        out_specs=pl.BlockSpec((1, seq, V_HEAD_DIM), lambda b, h: (b, 0, h)),
        out_shape=jax.ShapeDtypeStruct((batch, seq, V_DIM), bf),
        scratch_shapes=[
            pltpu.VMEM((seq, K_AUG), bf),
            pltpu.VMEM((V_AUG, seq), bf),
            pltpu.VMEM((nq, K_AUG, 2 * t), bf),
            pltpu.VMEM((nq + 1, V_AUG, 2 * t), f32),
            pltpu.VMEM((nq, 1, 2 * t), f32),
            pltpu.VMEM((t, 2 * t), f32),
            pltpu.VMEM((t, 2 * t), f32),
            pltpu.VMEM((t, 2 * t), bf),
            pltpu.VMEM((t, 2 * t), bf),
            pltpu.VMEM((1, 2 * t), f32),
            pltpu.VMEM((1, 2 * t), f32),
            pltpu.VMEM((t, 2 * t), f32),
            pltpu.VMEM((2 * t, 2 * t), f32),
            pltpu.VMEM((2 * t, 2 * t), f32),
            pltpu.VMEM((2 * t, 2 * t), bf),
            pltpu.VMEM((2 * t, 2 * t), bf),
        ],
        compiler_params=pltpu.CompilerParams(
            dimension_semantics=("arbitrary", "arbitrary"), vmem_limit_bytes=VMEM_LIMIT_BYTES),
        name="attn",
    )(slopes, fast, bound, lq, gq_col, gsub_row, qt, kn, vt)


def _mix_kernel(x_ref, u_ref, halo_ref, ya_ref, gate_ref, wpg_ref, ps_ref, wbp_ref, wba_ref,
                wout_ref, gffn_ref, wup_ref, wdown_ref, o_ref, *, tiles_per_seq):
    tm = TOKEN_TILE
    i = pl.program_id(0)
    tile_in_seq = i % tiles_per_seq

    halo = jnp.where(tile_in_seq == 0, 0.0, halo_ref[...].astype(jnp.float32))
    u = u_ref[...].astype(jnp.float32)
    ext = jnp.concatenate([halo, u], axis=0)
    pos = tile_in_seq * tm + lax.broadcasted_iota(jnp.int32, (tm, POOL_GROUP_DIM), 0)
    win = ext
    ys = []
    for g, w in enumerate(POOL_WINDOWS):
        win = win[:, POOL_GROUP_DIM * (1 if g else 0):]
        win = win + pltpu.roll(win, w // 2, 0)
        cnt = jnp.minimum(pos + 1, w).astype(jnp.float32)
        ug = u[:, g * POOL_GROUP_DIM:(g + 1) * POOL_GROUP_DIM]
        mixed = win[POOL_HALO:, :POOL_GROUP_DIM] / cnt - ug
        ys.append(jnp.dot(mixed.astype(jnp.bfloat16), wpg_ref[g],
                          preferred_element_type=jnp.float32))
    y_pool = (jnp.concatenate(ys, axis=1) * ps_ref[...]).astype(jnp.bfloat16)

    bp = jnp.dot(y_pool, wbp_ref[...], preferred_element_type=jnp.float32)
    ba = jnp.dot(ya_ref[...], wba_ref[...], preferred_element_type=jnp.float32)
    gates = gate_ref[...].astype(jnp.float32)
    merged = gates[:, :D_MODEL] * bp + gates[:, D_MODEL:] * ba
    x1 = x_ref[...] + jnp.dot(merged.astype(jnp.bfloat16), wout_ref[...],
                              preferred_element_type=jnp.float32)

    ms = jnp.mean(x1 * x1, axis=-1, keepdims=True)
    h2 = (x1 * lax.rsqrt(ms + EPS) * gffn_ref[...]).astype(jnp.bfloat16)
    y = x1
    for cidx in range(D_FF // FF_CHUNK):
        sl = slice(cidx * FF_CHUNK, (cidx + 1) * FF_CHUNK)
        up = jnp.maximum(jnp.dot(h2, wup_ref[:, sl], preferred_element_type=jnp.float32), 0.0)
        y = y + jnp.dot((up * up).astype(jnp.bfloat16), wdown_ref[sl, :],
                        preferred_element_type=jnp.float32)
    o_ref[...] = y


def _mix_call(layer, x2d, u, y_attn, gates, w_pg, pool_scale, w_bp, w_ba, w_out, g_ffn, w_up,
              w_down, seq):
    tokens = x2d.shape[0]
    tm = TOKEN_TILE
    row = lambda i: (i, 0)
    halo_blocks = tm // POOL_HALO
    kernel = functools.partial(_mix_kernel, tiles_per_seq=seq // tm)
    return pl.pallas_call(
        kernel,
        grid=(tokens // tm,),
        in_specs=[
            pl.BlockSpec((tm, D_MODEL), row),
            pl.BlockSpec((tm, POOL_DIM), row),
            pl.BlockSpec((POOL_HALO, POOL_DIM),
                         lambda i: (jnp.maximum(i * halo_blocks - 1, 0), 0)),
            pl.BlockSpec((tm, V_DIM), row),
            pl.BlockSpec((tm, 2 * D_MODEL), row),
            _layer_spec(w_pg, layer),
            _layer_spec(pool_scale, layer),
            _layer_spec(w_bp, layer),
            _layer_spec(w_ba, layer),
            _layer_spec(w_out, layer),
            _layer_spec(g_ffn, layer),
            _layer_spec(w_up, layer),
            _layer_spec(w_down, layer),
        ],
        out_specs=pl.BlockSpec((tm, D_MODEL), row),
        out_shape=jax.ShapeDtypeStruct((tokens, D_MODEL), jnp.float32),
        compiler_params=pltpu.CompilerParams(
            dimension_semantics=("arbitrary",), vmem_limit_bytes=VMEM_LIMIT_BYTES),
        name="mix",
    )(x2d, u, u, y_attn, gates, w_pg, pool_scale, w_bp, w_ba, w_out, g_ffn, w_up, w_down)


def kernel(x, g_mix, w_in, w_pool_grp, pool_scale, g_q, g_k, lambda_qk, g_sub, w_branch_pool,
           w_branch_attn, w_gate, b_gate, w_out, g_ffn, w_up, w_down):
    batch, seq, d_model = x.shape
    depth = w_in.shape[0]
    assert d_model == D_MODEL and seq % TOKEN_TILE == 0 and seq % ATTN_TILE == 0
    assert TOKEN_TILE % POOL_HALO == 0 and POOL_HALO >= max(POOL_WINDOWS) - 1
    bf = jnp.bfloat16
    f32 = jnp.float32

    slopes = jnp.asarray([2.0 ** (-8.0 * (i + 1) / N_HEADS) for i in range(N_HEADS)], f32)
    grp = jnp.arange(MXU_COLS_V7X) // HEAD_DIM
    e_blk = (grp[:, None] == grp[None, :]).astype(bf)

    lambda_inits = [0.8 - 0.6 * math.exp(-0.3 * l) for l in range(depth)]
    k_lo, v_lo = POOL_DIM + QK_DIM, POOL_DIM + 2 * QK_DIM
    w_pk = jnp.concatenate([w_in[:, :, :POOL_DIM], w_in[:, :, k_lo:v_lo]], axis=2).astype(bf)
    w_qvt = jnp.concatenate([w_in[:, :, POOL_DIM:k_lo], w_in[:, :, v_lo:]], axis=2)
    w_qvt = jnp.swapaxes(w_qvt, 1, 2).astype(bf)
    row_vec = lambda a: a.astype(f32).reshape(depth, 1, -1)
    gk_row = row_vec(jnp.tile(g_k, (1, 2 * N_HEADS)))
    gk_max = jnp.max(jnp.abs(g_k.astype(f32)), axis=1)
    gq_col = (g_q.astype(f32) * (HEAD_DIM ** -0.5 * LOG2E)).reshape(depth, HEAD_DIM, 1)
    gsub_row = row_vec(g_sub) * (1.0 - jnp.asarray(lambda_inits, f32)).reshape(depth, 1, 1)
    g_mix_r, b_gate_r, pool_scale_r, g_ffn_r = map(row_vec, (g_mix, b_gate, pool_scale, g_ffn))
    lq = lambda_qk.astype(f32)
    w_gate_b, w_pg_b, w_bp_b, w_ba_b, w_out_b, w_up_b, w_down_b = (
        a.astype(bf) for a in (w_gate, w_pool_grp, w_branch_pool, w_branch_attn, w_out, w_up, w_down))

    x2d = x.reshape(batch * seq, d_model)
    for l in range(depth):
        u, kn, qt, vt, gates = _proj_call(l, x2d, g_mix_r, w_pk, w_qvt, w_gate_b, b_gate_r, gk_row,
                                          e_blk, batch, seq)
        y_attn = _attn_call(l, slopes, lq, gq_col, gk_max, gsub_row, qt,
                            kn.reshape(batch, seq, QK_DIM), vt, lambda_inits[l])
        x2d = _mix_call(l, x2d, u, y_attn.reshape(batch * seq, V_DIM), gates, w_pg_b, pool_scale_r,
                        w_bp_b, w_ba_b, w_out_b, g_ffn_r, w_up_b, w_down_b, seq)
    return x2d.reshape(batch, seq, d_model)
```
